```python
import math
import jax, jax.numpy as jnp
from jax import lax
import numpy as np

D_MODEL = 1024
BATCH = 8
SEQ = 2048
DEPTH = 1
DEC_BATCH = 8
DEC_SEQ = 8192
PAST_LEN = 128

HG_HEADS = 4
HG_HEAD_DIM = 128
HG_WIDTH = HG_HEADS * HG_HEAD_DIM
HG_CHUNK = 64
HG_SCALE = HG_HEAD_DIM ** -0.5
DA_HEADS = 4
DA_HEAD_DIM = 64
DA_V_DIM = 2 * DA_HEAD_DIM
DA_QK_WIDTH = DA_HEADS * 2 * DA_HEAD_DIM
DA_V_WIDTH = DA_HEADS * DA_V_DIM
DA_SCALE = DA_HEAD_DIM ** -0.5
Q_BLOCK = 128
ROT_DIM = DA_HEAD_DIM // 4
ROPE_THETA = 500000.0
D_FF = 4 * D_MODEL
NORM_EPS = 1e-6
SUBLN_EPS = 1e-5
IN_SIZES = (DA_QK_WIDTH, DA_QK_WIDTH, DA_V_WIDTH,
            HG_WIDTH, HG_WIDTH, HG_WIDTH, HG_WIDTH, HG_WIDTH,
            D_MODEL, D_MODEL)
IN_WIDTH = 3 * 512 + 5 * HG_WIDTH + 2 * D_MODEL

kernel_name = 'hgrn2_diffattn_parallel_encoder'

F32 = jnp.float32


def _rmsnorm(x, g, eps=NORM_EPS):
    xf = x.astype(F32)
    y = xf * lax.rsqrt(jnp.mean(xf * xf, axis=-1, keepdims=True) + eps) * g.astype(F32)
    return y.astype(x.dtype)


def _gla_chunkwise(q, k, v, log_f):
    B, H, T, dk = q.shape
    dv = v.shape[-1]
    n = T // HG_CHUNK

    def chunks(a):
        return a.reshape(B, H, n, HG_CHUNK, a.shape[-1]).transpose(2, 0, 1, 3, 4)

    qc, kc, vc = chunks(q), chunks(k), chunks(v)
    gc = jnp.cumsum(chunks(log_f), axis=3)
    mask = jnp.tril(jnp.ones((HG_CHUNK, HG_CHUNK), dtype=bool))[:, :, None]

    def step(S, inp):
        q_, k_, v_, g_ = inp
        diff = g_[:, :, :, None, :] - g_[:, :, None, :, :]
        decay = jnp.exp(jnp.where(mask, diff, -jnp.inf))
        a = jnp.einsum('bhtk,bhsk,bhtsk->bhts', q_, k_, decay)
        o = (jnp.einsum('bhts,bhsv->bhtv', a, v_)
             + jnp.einsum('bhtk,bhkv->bhtv', q_ * jnp.exp(g_), S))
        g_last = g_[:, :, -1:, :]
        S = (S * jnp.exp(g_last[:, :, 0, :])[..., None]
             + jnp.einsum('bhsk,bhsv->bhkv', k_ * jnp.exp(g_last - g_), v_))
        return S, o

    S0 = jnp.zeros((B, H, dk, dv), F32)
    _, o = lax.scan(step, S0, (qc, kc, vc, gc))
    return o.transpose(1, 2, 0, 3, 4).reshape(B, H, T, dv)


def _partial_rotary(x, pos):
    half = ROT_DIM // 2
    inv_freq = ROPE_THETA ** (-jnp.arange(0, ROT_DIM, 2, dtype=F32) / ROT_DIM)
    ang = pos[:, None] * inv_freq[None, :]
    cos, sin = jnp.cos(ang), jnp.sin(ang)
    x1, x2, xp = x[..., :half], x[..., half:ROT_DIM], x[..., ROT_DIM:]
    return jnp.concatenate([x1 * cos - x2 * sin, x2 * cos + x1 * sin, xp], axis=-1)


def _diff_attention(q, k, v, lam):
    B, H2, T, d = q.shape
    H = H2 // 2
    nb = T // Q_BLOCK
    qb = q.reshape(B, H2, nb, Q_BLOCK, d).transpose(2, 0, 1, 3, 4)

    def block(qi):
        s = jnp.einsum('bhqd,bhkd->bhqk', qi, k) * DA_SCALE
        p = jax.nn.softmax(s, axis=-1).reshape(B, H, 2, Q_BLOCK, T)
        w = p[:, :, 0] - lam * p[:, :, 1]
        return jnp.einsum('bhqk,bhkv->bhqv', w, v)

    o = lax.map(block, qb)
    return o.transpose(1, 2, 0, 3, 4).reshape(B, H, T, v.shape[-1])


def _layer(x, pos, lidx, norm1, w_in, hg_lb_logits, hg_norm, w_hg_branch,
           lq1, lk1, lq2, lk2, da_subln, w_da_branch, w_out, norm2, w_mlp_in, w_mlp_out):
    B, T, _ = x.shape
    h = _rmsnorm(x, norm1)
    u = h @ w_in
    offs = np.cumsum(np.array(IN_SIZES))[:-1].tolist()
    q_da, k_da, v_da, q_hg, f_fw, f_bw, i_hg, g_hg, gate_hg, gate_da = jnp.split(u, offs, axis=-1)

    lb = jnp.cumsum(jax.nn.softmax(hg_lb_logits.astype(F32), axis=1), axis=1)[:, lidx]
    ff = lb[0] + (1.0 - lb[0]) * jax.nn.sigmoid(f_fw.astype(F32))
    fb = lb[1] + (1.0 - lb[1]) * jax.nn.sigmoid(f_bw.astype(F32))

    def heads(a):
        return a.reshape(B, T, HG_HEADS, HG_HEAD_DIM).transpose(0, 2, 1, 3)

    def rev(a):
        return jnp.flip(a, axis=2)

    q = heads(jax.nn.silu(q_hg.astype(F32)) * HG_SCALE)
    i = heads(i_hg.astype(F32))
    ff, fb = heads(ff), heads(fb)
    o = _gla_chunkwise(jnp.concatenate([q, rev(q)], axis=1),
                       jnp.concatenate([1.0 - ff, rev(1.0 - fb)], axis=1),
                       jnp.concatenate([i, rev(i)], axis=1),
                       jnp.concatenate([jnp.log(ff), rev(jnp.log(fb))], axis=1))
    o = o[:, :HG_HEADS] + rev(o[:, HG_HEADS:])
    o = _rmsnorm(o.transpose(0, 2, 1, 3), hg_norm.reshape(HG_HEADS, HG_HEAD_DIM))
    o = o.reshape(B, T, HG_WIDTH) * jax.nn.silu(g_hg.astype(F32))
    y_hg = o.astype(x.dtype) @ w_hg_branch

    lam_init = 0.8 - 0.6 * math.exp(-0.3 * lidx)
    lam = (jnp.exp(jnp.sum(lq1.astype(F32) * lk1.astype(F32)))
           - jnp.exp(jnp.sum(lq2.astype(F32) * lk2.astype(F32))) + lam_init)
    qa = q_da.astype(F32).reshape(B, T, 2 * DA_HEADS, DA_HEAD_DIM).transpose(0, 2, 1, 3)
    ka = k_da.astype(F32).reshape(B, T, 2 * DA_HEADS, DA_HEAD_DIM).transpose(0, 2, 1, 3)
    va = v_da.astype(F32).reshape(B, T, DA_HEADS, DA_V_DIM).transpose(0, 2, 1, 3)
    oa = _diff_attention(_partial_rotary(qa, pos), _partial_rotary(ka, pos), va, lam)
    oa = _rmsnorm(oa, da_subln, SUBLN_EPS) * (1.0 - lam_init)
    oa = oa.transpose(0, 2, 1, 3).reshape(B, T, DA_V_WIDTH)
    y_da = oa.astype(x.dtype) @ w_da_branch

    m = (jax.nn.sigmoid(gate_hg.astype(F32)) * y_hg.astype(F32)
         + jax.nn.sigmoid(gate_da.astype(F32)) * y_da.astype(F32))
    x = x + m.astype(x.dtype) @ w_out

    h2 = _rmsnorm(x, norm2)
    x = x + jnp.square(jax.nn.relu(h2 @ w_mlp_in)) @ w_mlp_out
    return x


def _trunk(x, norm1, w_in, hg_lb_logits, hg_norm, w_hg_branch, da_lambda_q1, da_lambda_k1,
           da_lambda_q2, da_lambda_k2, da_subln, w_da_branch, w_out, norm2, w_mlp_in,
           w_mlp_out, final_norm):
    pos = jnp.arange(x.shape[1], dtype=F32)
    for l in range(DEPTH):
        x = _layer(x, pos, l, norm1[l], w_in[l], hg_lb_logits, hg_norm[l], w_hg_branch[l],
                   da_lambda_q1[l], da_lambda_k1[l], da_lambda_q2[l], da_lambda_k2[l],
                   da_subln[l], w_da_branch[l], w_out[l], norm2[l], w_mlp_in[l], w_mlp_out[l])
    return _rmsnorm(x, final_norm)


def setup_inputs(seed: int = 0) -> dict:
    key = jax.random.key(seed)
    ks = jax.random.split(key, 18)

    def nrm(k, shape, scale):
        return jax.random.normal(k, shape, F32) * scale

    return {
        'x_prompt': nrm(ks[0], (BATCH, SEQ, D_MODEL), 1.0),
        'x_sample': nrm(ks[1], (DEC_BATCH, DEC_SEQ, D_MODEL), 1.0),
        'norm1': 1.0 + nrm(ks[2], (DEPTH, D_MODEL), 0.02),
        'w_in': nrm(ks[3], (DEPTH, D_MODEL, IN_WIDTH), D_MODEL ** -0.5),
        'hg_lb_logits': nrm(ks[4], (2, DEPTH + 1, HG_WIDTH), 1.0),
        'hg_norm': 1.0 + nrm(ks[5], (DEPTH, HG_WIDTH), 0.02),
        'w_hg_branch': nrm(ks[6], (DEPTH, HG_WIDTH, D_MODEL), HG_WIDTH ** -0.5),
        'da_lambda_q1': nrm(ks[7], (DEPTH, DA_HEAD_DIM), 0.1),
        'da_lambda_k1': nrm(ks[8], (DEPTH, DA_HEAD_DIM), 0.1),
        'da_lambda_q2': nrm(ks[9], (DEPTH, DA_HEAD_DIM), 0.1),
        'da_lambda_k2': nrm(ks[10], (DEPTH, DA_HEAD_DIM), 0.1),
        'da_subln': 1.0 + nrm(ks[11], (DEPTH, DA_V_DIM), 0.02),
        'w_da_branch': nrm(ks[12], (DEPTH, DA_V_WIDTH, D_MODEL), DA_V_WIDTH ** -0.5),
        'w_out': nrm(ks[13], (DEPTH, D_MODEL, D_MODEL), D_MODEL ** -0.5),
        'norm2': 1.0 + nrm(ks[14], (DEPTH, D_MODEL), 0.02),
        'w_mlp_in': nrm(ks[15], (DEPTH, D_MODEL, D_FF), D_MODEL ** -0.5),
        'w_mlp_out': nrm(ks[16], (DEPTH, D_FF, D_MODEL), D_FF ** -0.5),
        'final_norm': 1.0 + nrm(ks[17], (D_MODEL,), 0.02),
    }


def reference(x_prompt, x_sample, norm1, w_in, hg_lb_logits, hg_norm, w_hg_branch,
              da_lambda_q1, da_lambda_k1, da_lambda_q2, da_lambda_k2, da_subln, w_da_branch,
              w_out, norm2, w_mlp_in, w_mlp_out, final_norm):
    y_prompt = _trunk(x_prompt, norm1, w_in, hg_lb_logits, hg_norm, w_hg_branch, da_lambda_q1,
                      da_lambda_k1, da_lambda_q2, da_lambda_k2, da_subln, w_da_branch, w_out,
                      norm2, w_mlp_in, w_mlp_out, final_norm)
    y_sample = _trunk(x_sample, norm1, w_in, hg_lb_logits, hg_norm, w_hg_branch, da_lambda_q1,
                      da_lambda_k1, da_lambda_q2, da_lambda_k2, da_subln, w_da_branch, w_out,
                      norm2, w_mlp_in, w_mlp_out, final_norm)
    return (y_prompt, y_sample)
```

```python
import functools
import math

import jax
import jax.numpy as jnp
from jax import lax
from jax.experimental import pallas as pl
from jax.experimental.pallas import tpu as pltpu

F32 = jnp.float32
BF16 = jnp.bfloat16

D_MODEL = 1024
HG_HEADS = 4
HG_HEAD_DIM = 128
HG_WIDTH = HG_HEADS * HG_HEAD_DIM
HG_SCALE = HG_HEAD_DIM ** -0.5
DA_HEADS = 4
DA_HEAD_DIM = 64
DA_V_DIM = 2 * DA_HEAD_DIM
DA_WIDTH = DA_HEADS * DA_V_DIM
DA_SCALE = DA_HEAD_DIM ** -0.5
ROT_DIM = DA_HEAD_DIM // 4
ROT_HALF = ROT_DIM // 2
ROPE_THETA = 500000.0
D_FF = 4 * D_MODEL
NORM_EPS = 1e-6
SUBLN_EPS = 1e-5
IN_WIDTH = 3 * DA_WIDTH + 5 * HG_WIDTH + 2 * D_MODEL
LAM_INIT = 0.8 - 0.6 * math.exp(-0.3 * 0)

LANES = 128
SUBLANES = 8
VMEM_LIMIT = 56 * 1024 * 1024

SEG = 512
IN_ROWS = 512
ATT_Q = 128
HG_CHUNK = 128
MLP_ROWS = 256
MLP_COLS = 1024

_NT = (((1,), (1,)), ((), ()))
_TN = (((0,), (0,)), ((), ()))


def _sigmoid(x):
    return 1.0 / (1.0 + jnp.exp(-x))


def _const_spec(shape):
    nd = len(shape)
    return pl.BlockSpec(shape, lambda *_: (0,) * nd, pipeline_mode=pl.Buffered(1))


def _inproj_kernel(x_ref, n1_ref, w_ref, lbl_ref, rot_ref,
                   qd_ref, kd_ref, vd_ref, qh_ref, ff_ref, fb_ref, ih_ref, gh_ref, sgh_ref, sgd_ref):
    x = x_ref[...]
    ms = jnp.mean(x * x, axis=-1, keepdims=True)
    h = (x * lax.rsqrt(ms + NORM_EPS) * n1_ref[...]).astype(BF16)

    def seg(j):
        return jnp.dot(h, w_ref[:, j * SEG:(j + 1) * SEG], preferred_element_type=F32)

    cos_t = rot_ref[:, 0:LANES]
    sin_lo = rot_ref[:, LANES:2 * LANES]
    sin_hi = rot_ref[:, 2 * LANES:3 * LANES]

    def rope_store(u, out_ref, scale):
        for p in range(SEG // LANES):
            a = u[:, p * LANES:(p + 1) * LANES]
            r = (a * cos_t + pltpu.roll(a, ROT_HALF, 1) * sin_lo
                 + pltpu.roll(a, LANES - ROT_HALF, 1) * sin_hi)
            if scale != 1.0:
                r = r * scale
            out_ref[:, p * LANES:(p + 1) * LANES] = r.astype(out_ref.dtype)

    rope_store(seg(0), qd_ref, DA_SCALE)
    rope_store(seg(1), kd_ref, 1.0)
    vd_ref[...] = seg(2).astype(BF16)

    u = seg(3)
    qh_ref[...] = (u * _sigmoid(u) * HG_SCALE).astype(BF16)

    def lower_bound(d):
        l0 = lbl_ref[2 * d:2 * d + 1, :]
        l1 = lbl_ref[2 * d + 1:2 * d + 2, :]
        m = jnp.maximum(l0, l1)
        e0 = jnp.exp(l0 - m)
        e1 = jnp.exp(l1 - m)
        return e0 / (e0 + e1)

    lb = lower_bound(0)
    ff_ref[...] = lb + (1.0 - lb) * _sigmoid(seg(4))
    lb = lower_bound(1)
    fb_ref[...] = lb + (1.0 - lb) * _sigmoid(seg(5))
    ih_ref[...] = seg(6).astype(BF16)
    u = seg(7)
    gh_ref[...] = (u * _sigmoid(u)).astype(BF16)
    for p in range(D_MODEL // SEG):
        sgh_ref[:, p * SEG:(p + 1) * SEG] = _sigmoid(seg(8 + p)).astype(BF16)
        sgd_ref[:, p * SEG:(p + 1) * SEG] = _sigmoid(seg(10 + p)).astype(BF16)


def _inproj(x2, n1, w_in, lbl, rot, seq):
    n = x2.shape[0]
    tm = min(IN_ROWS, seq)
    nt = seq // tm
    row = lambda w: pl.BlockSpec((tm, w), lambda i: (i, 0))
    out_w = (DA_WIDTH, DA_WIDTH, DA_WIDTH, HG_WIDTH, HG_WIDTH, HG_WIDTH, HG_WIDTH, HG_WIDTH, D_MODEL, D_MODEL)
    out_dt = (BF16, BF16, BF16, BF16, F32, F32, BF16, BF16, BF16, BF16)
    return pl.pallas_call(
        _inproj_kernel,
        grid=(n // tm,),
        in_specs=[row(D_MODEL), _const_spec((1, D_MODEL)), _const_spec((D_MODEL, IN_WIDTH)),
                  _const_spec((4, HG_WIDTH)),
                  pl.BlockSpec((tm, 3 * LANES), lambda i: (i % nt, 0))],
        out_specs=[row(w) for w in out_w],
        out_shape=[jax.ShapeDtypeStruct((n, w), dt) for w, dt in zip(out_w, out_dt)],
        compiler_params=pltpu.CompilerParams(dimension_semantics=("parallel",),
                                             vmem_limit_bytes=VMEM_LIMIT),
        name="inproj",
    )(x2, n1, w_in, lbl, rot)


def _attn_kernel(lam_ref, sub_ref, q_ref, k_ref, v_ref, o_ref):
    lv = lam_ref[...]
    lam = (jnp.exp(jnp.sum(lv[0:1] * lv[1:2], axis=-1, keepdims=True))
           - jnp.exp(jnp.sum(lv[2:3] * lv[3:4], axis=-1, keepdims=True)) + LAM_INIT)
    q = q_ref[...]
    lane = lax.broadcasted_iota(jnp.int32, q.shape, 1)
    zero = jnp.zeros_like(q)
    q1 = jnp.where(lane < DA_HEAD_DIM, q, zero)
    q2 = jnp.where(lane >= DA_HEAD_DIM, q, zero)
    k = k_ref[...]

    def softmax_parts(qc):
        s = lax.dot_general(qc, k, _NT, preferred_element_type=F32)
        p = jnp.exp(s - jnp.max(s, axis=-1, keepdims=True))
        return p, jnp.sum(p, axis=-1, keepdims=True)

    p1, l1 = softmax_parts(q1)
    p2, l2 = softmax_parts(q2)
    w = p1 * (1.0 / l1) - p2 * (lam / l2)
    o = jnp.dot(w.astype(BF16), v_ref[...], preferred_element_type=F32)
    ms = jnp.mean(o * o, axis=-1, keepdims=True)
    o = o * lax.rsqrt(ms + SUBLN_EPS) * sub_ref[...] * (1.0 - LAM_INIT)
    o_ref[...] = o.astype(o_ref.dtype)


def _attention(lamv, subln, qd, kd, vd, batch, seq):
    n = qd.shape[0]
    tq = min(ATT_Q, seq)
    nq = seq // tq
    return pl.pallas_call(
        _attn_kernel,
        grid=(batch, DA_HEADS, nq),
        in_specs=[pl.BlockSpec((4, DA_HEAD_DIM), lambda b, h, i: (0, 0)),
                  pl.BlockSpec((1, DA_V_DIM), lambda b, h, i: (0, 0)),
                  pl.BlockSpec((tq, DA_V_DIM), lambda b, h, i: (b * nq + i, h)),
                  pl.BlockSpec((seq, DA_V_DIM), lambda b, h, i: (b, h)),
                  pl.BlockSpec((seq, DA_V_DIM), lambda b, h, i: (b, h))],
        out_specs=pl.BlockSpec((tq, DA_V_DIM), lambda b, h, i: (b * nq + i, h)),
        out_shape=jax.ShapeDtypeStruct((n, DA_WIDTH), BF16),
        compiler_params=pltpu.CompilerParams(
            dimension_semantics=("parallel", "parallel", "arbitrary"), vmem_limit_bytes=VMEM_LIMIT),
        name="diffattn",
    )(lamv, subln, qd, kd, vd)


def _cumsum_rows(x, rev):
    c = x.shape[0]
    row = lax.broadcasted_iota(jnp.int32, x.shape, 0)
    sh = 1
    while sh < c:
        if rev:
            x = x + jnp.where(row < c - sh, pltpu.roll(x, c - sh, 0), 0.0)
        else:
            x = x + jnp.where(row >= sh, pltpu.roll(x, sh, 0), 0.0)
        sh *= 2
    return x


def _block_row(g, half, rev):
    c, w = g.shape
    idx = half if rev else half - 1
    blk = 2 * half
    if blk >= SUBLANES:
        g3 = g.reshape(c // blk, blk, w)
        return jnp.broadcast_to(g3[:, idx:idx + 1, :], g3.shape).reshape(c, w)
    g3 = g.reshape(c // SUBLANES, SUBLANES, w)
    sub = lax.broadcasted_iota(jnp.int32, g3.shape, 1)
    out = None
    for b in reversed(range(SUBLANES // blk)):
        piece = jnp.broadcast_to(g3[:, b * blk + idx:b * blk + idx + 1, :], g3.shape)
        out = piece if out is None else jnp.where(sub < (b + 1) * blk, piece, out)
    return out.reshape(c, w)


def _gla_chunk(q, f, v, st_ref, pair_level, rev):
    c = q.shape[0]
    qf = q.astype(F32)
    vf = v.astype(F32)
    kk = 1.0 - f
    g = _cumsum_rows(jnp.log(f), rev)
    row = lax.broadcasted_iota(jnp.int32, (c, 1), 0)

    st = st_ref[...]
    o = lax.dot_general((qf * jnp.exp(g)).astype(BF16), st.astype(BF16), _NT,
                        preferred_element_type=F32)
    o = o + jnp.sum(qf * kk, axis=-1, keepdims=True) * vf

    a = jnp.zeros((c, c), F32)
    half = 1
    lvl = 0
    while half < c:
        d = jnp.exp(-jnp.abs(g - _block_row(g, half, rev)))
        second = (row & half) != 0
        tgt = jnp.logical_not(second) if rev else second
        qt = jnp.where(tgt, qf * d, 0.0).astype(BF16)
        kt = jnp.where(tgt, 0.0, kk * d).astype(BF16)
        al = lax.dot_general(qt, kt, _NT, preferred_element_type=F32)
        a = jnp.where(pair_level == lvl, al, a)
        half *= 2
        lvl += 1
    o = o + jnp.dot(a.astype(BF16), v, preferred_element_type=F32)

    g_tot = g[0:1, :] if rev else g[c - 1:c, :]
    ks = (kk * jnp.exp(g_tot - g)).astype(BF16)
    st_ref[...] = st * jnp.exp(g_tot) + lax.dot_general(v, ks, _TN, preferred_element_type=F32)
    return o


def _hgrn_kernel(qf_ref, ff_ref, vf_ref, qb_ref, fb_ref, vb_ref, of_ref, ob_ref, st_ref):
    @pl.when(pl.program_id(1) == 0)
    def _():
        st_ref[...] = jnp.zeros_like(st_ref)

    c = qf_ref.shape[0]
    t = lax.broadcasted_iota(jnp.int32, (c, c), 0)
    s = lax.broadcasted_iota(jnp.int32, (c, c), 1)
    x = t ^ s
    def levels(xc):
        lv = jnp.full((c, c), -1, jnp.int32)
        half = 1
        i = 0
        while half < c:
            lv = jnp.where((xc >> i) == 1, i, lv)
            half *= 2
            i += 1
        return lv
    lv_fwd = levels(jnp.where(t > s, x, 0))
    lv_bwd = levels(jnp.where(t < s, x, 0))

    for h in range(HG_HEADS):
        sl = slice(h * HG_HEAD_DIM, (h + 1) * HG_HEAD_DIM)
        of_ref[:, sl] = _gla_chunk(qf_ref[:, sl], ff_ref[:, sl], vf_ref[:, sl], st_ref.at[h], lv_fwd, False)
        ob_ref[:, sl] = _gla_chunk(qb_ref[:, sl], fb_ref[:, sl], vb_ref[:, sl], st_ref.at[HG_HEADS + h],
                                   lv_bwd, True)


def _hgrn(qh, ff, fb, ih, batch, seq):
    n = qh.shape[0]
    c = min(HG_CHUNK, seq)
    nc = seq // c
    fwd = pl.BlockSpec((c, HG_WIDTH), lambda b, j: (b * nc + j, 0))
    bwd = pl.BlockSpec((c, HG_WIDTH), lambda b, j: (b * nc + nc - 1 - j, 0))
    return pl.pallas_call(
        _hgrn_kernel,
        grid=(batch, nc),
        in_specs=[fwd, fwd, fwd, bwd, bwd, bwd],
        out_specs=[fwd, bwd],
        out_shape=[jax.ShapeDtypeStruct((n, HG_WIDTH), F32)] * 2,
        scratch_shapes=[pltpu.VMEM((2 * HG_HEADS, HG_HEAD_DIM, HG_HEAD_DIM), F32)],
        compiler_params=pltpu.CompilerParams(dimension_semantics=("parallel", "arbitrary"),
                                             vmem_limit_bytes=VMEM_LIMIT),
        name="hgrn2",
    )(qh, ff, ih, qh, fb, ih)


def _merge_mlp_kernel(x_ref, of_ref, ob_ref, gh_ref, oa_ref, sgh_ref, sgd_ref,
                      hgn_ref, n2_ref, fn_ref, whg_ref, wda_ref, wout_ref, w1_ref, w2_ref, y_ref):
    o = of_ref[...] + ob_ref[...]
    parts = []
    for h in range(HG_HEADS):
        sl = slice(h * HG_HEAD_DIM, (h + 1) * HG_HEAD_DIM)
        oh = o[:, sl]
        ms = jnp.mean(oh * oh, axis=-1, keepdims=True)
        parts.append(oh * lax.rsqrt(ms + NORM_EPS) * hgn_ref[:, sl])
    o = jnp.concatenate(parts, axis=1) * gh_ref[...].astype(F32)
    y_hg = jnp.dot(o.astype(BF16), whg_ref[...], preferred_element_type=F32)
    y_da = jnp.dot(oa_ref[...], wda_ref[...], preferred_element_type=F32)
    m = sgh_ref[...].astype(F32) * y_hg + sgd_ref[...].astype(F32) * y_da
    x1 = x_ref[...] + jnp.dot(m.astype(BF16), wout_ref[...], preferred_element_type=F32)

    ms = jnp.mean(x1 * x1, axis=-1, keepdims=True)
    h2 = (x1 * lax.rsqrt(ms + NORM_EPS) * n2_ref[...]).astype(BF16)
    acc = x1
    for j in range(D_FF // MLP_COLS):
        cs = slice(j * MLP_COLS, (j + 1) * MLP_COLS)
        mid = jnp.maximum(jnp.dot(h2, w1_ref[:, cs], preferred_element_type=F32), 0.0)
        acc = acc + jnp.dot((mid * mid).astype(BF16), w2_ref[cs, :], preferred_element_type=F32)
    ms = jnp.mean(acc * acc, axis=-1, keepdims=True)
    y_ref[...] = acc * lax.rsqrt(ms + NORM_EPS) * fn_ref[...]


def _merge_mlp(x2, of, ob, gh, oa, sgh, sgd, hgn, n2, fn, whg, wda, wout, w1, w2):
    n = x2.shape[0]
    tm = MLP_ROWS
    row = lambda w: pl.BlockSpec((tm, w), lambda i: (i, 0))
    return pl.pallas_call(
        _merge_mlp_kernel,
        grid=(n // tm,),
        in_specs=[row(D_MODEL), row(HG_WIDTH), row(HG_WIDTH), row(HG_WIDTH), row(DA_WIDTH),
                  row(D_MODEL), row(D_MODEL),
                  _const_spec((1, HG_WIDTH)), _const_spec((1, D_MODEL)), _const_spec((1, D_MODEL)),
                  _const_spec((HG_WIDTH, D_MODEL)), _const_spec((DA_WIDTH, D_MODEL)),
                  _const_spec((D_MODEL, D_MODEL)), _const_spec((D_MODEL, D_FF)),
                  _const_spec((D_FF, D_MODEL))],
        out_specs=row(D_MODEL),
        out_shape=jax.ShapeDtypeStruct((n, D_MODEL), F32),
        compiler_params=pltpu.CompilerParams(dimension_semantics=("parallel",),
                                             vmem_limit_bytes=VMEM_LIMIT),
        name="merge_mlp",
    )(x2, of, ob, gh, oa, sgh, sgd, hgn, n2, fn, whg, wda, wout, w1, w2)


def _rotary_table(seq):
    pos = jnp.arange(seq, dtype=F32)
    inv_freq = ROPE_THETA ** (-jnp.arange(0, ROT_DIM, 2, dtype=F32) / ROT_DIM)
    ang = pos[:, None] * inv_freq[None, :]
    cos, sin = jnp.cos(ang), jnp.sin(ang)
    pad = jnp.zeros((seq, DA_HEAD_DIM - ROT_DIM), F32)
    zeros = jnp.zeros_like(sin)
    cos64 = jnp.concatenate([cos, cos, pad + 1.0], axis=1)
    lo64 = jnp.concatenate([zeros, sin, pad], axis=1)
    hi64 = jnp.concatenate([-sin, zeros, pad], axis=1)
    return jnp.concatenate([cos64, cos64, lo64, lo64, hi64, hi64], axis=1)


def _trunk(x, p):
    batch, seq, _ = x.shape
    assert seq % max(IN_ROWS, ATT_Q, HG_CHUNK, MLP_ROWS) == 0
    x2 = x.reshape(batch * seq, D_MODEL)
    qd, kd, vd, qh, ff, fb, ih, gh, sgh, sgd = _inproj(x2, p["n1"], p["w_in"], p["lbl"],
                                                       _rotary_table(seq), seq)
    oa = _attention(p["lamv"], p["subln"], qd, kd, vd, batch, seq)
    of, ob = _hgrn(qh, ff, fb, ih, batch, seq)
    y = _merge_mlp(x2, of, ob, gh, oa, sgh, sgd, p["hgn"], p["n2"], p["fn"],
                   p["whg"], p["wda"], p["wout"], p["w1"], p["w2"])
    return y.reshape(batch, seq, D_MODEL)


def kernel(x_prompt, x_sample, norm1, w_in, hg_lb_logits, hg_norm, w_hg_branch, da_lambda_q1, da_lambda_k1, da_lambda_q2, da_lambda_k2, da_subln, w_da_branch, w_out, norm2, w_mlp_in, w_mlp_out, final_norm):
    assert norm1.shape[0] == 1 and hg_lb_logits.shape == (2, 2, HG_WIDTH)
    p = dict(
        n1=norm1.reshape(1, D_MODEL), w_in=w_in[0].astype(BF16),
        lbl=hg_lb_logits.reshape(4, HG_WIDTH),
        lamv=jnp.concatenate([da_lambda_q1, da_lambda_k1, da_lambda_q2, da_lambda_k2], axis=0),
        subln=da_subln.reshape(1, DA_V_DIM), hgn=hg_norm.reshape(1, HG_WIDTH),
        n2=norm2.reshape(1, D_MODEL), fn=final_norm.reshape(1, D_MODEL),
        whg=w_hg_branch[0].astype(BF16), wda=w_da_branch[0].astype(BF16), wout=w_out[0].astype(BF16),
        w1=w_mlp_in[0].astype(BF16), w2=w_mlp_out[0].astype(BF16),
    )
    return _trunk(x_prompt, p), _trunk(x_sample, p)
```

```python
import functools
import math

import jax
import jax.numpy as jnp
from jax import lax
from jax.experimental import pallas as pl
from jax.experimental.pallas import tpu as pltpu

F32 = jnp.float32
BF16 = jnp.bfloat16

D_MODEL = 1024
HG_HEADS = 4
HG_HEAD_DIM = 128
HG_WIDTH = HG_HEADS * HG_HEAD_DIM
HG_SCALE = HG_HEAD_DIM ** -0.5
DA_HEADS = 4
DA_HEAD_DIM = 64
DA_V_DIM = 2 * DA_HEAD_DIM
DA_WIDTH = DA_HEADS * DA_V_DIM
DA_SCALE = DA_HEAD_DIM ** -0.5
ROT_DIM = DA_HEAD_DIM // 4
ROT_HALF = ROT_DIM // 2
ROPE_THETA = 500000.0
D_FF = 4 * D_MODEL
NORM_EPS = 1e-6
SUBLN_EPS = 1e-5
IN_WIDTH = 3 * DA_WIDTH + 5 * HG_WIDTH + 2 * D_MODEL
LAM_INIT = 0.8 - 0.6 * math.exp(-0.3 * 0)

LANES = 128
SUBLANES = 8
VMEM_LIMIT = 56 * 1024 * 1024

SEG = 512
IN_ROWS = 512
ATT_Q = 128
ATT_STREAMS = 2
ATT_K = 512
LOG2_E = math.log2(math.e)
HG_CHUNK = 128
MLP_ROWS = 256
MLP_COLS = 1024

_NT = (((1,), (1,)), ((), ()))
_TN = (((0,), (0,)), ((), ()))


def _sigmoid(x):
    return 1.0 / (1.0 + jnp.exp(-x))


def _const_spec(shape):
    nd = len(shape)
    return pl.BlockSpec(shape, lambda *_: (0,) * nd, pipeline_mode=pl.Buffered(1))


def _inproj_kernel(x_ref, n1_ref, w_ref, lbl_ref, rot_ref,
                   qd_ref, kd_ref, vd_ref, qh_ref, ff_ref, fb_ref, ih_ref, gh_ref, sgh_ref, sgd_ref):
    x = x_ref[...]
    ms = jnp.mean(x * x, axis=-1, keepdims=True)
    h = (x * lax.rsqrt(ms + NORM_EPS) * n1_ref[...]).astype(BF16)

    def seg(j):
        return jnp.dot(h, w_ref[:, j * SEG:(j + 1) * SEG], preferred_element_type=F32)

    cos_t = rot_ref[:, 0:LANES]
    sin_lo = rot_ref[:, LANES:2 * LANES]
    sin_hi = rot_ref[:, 2 * LANES:3 * LANES]

    def rope_store(u, out_ref, scale):
        for p in range(SEG // LANES):
            a = u[:, p * LANES:(p + 1) * LANES]
            r = (a * cos_t + pltpu.roll(a, ROT_HALF, 1) * sin_lo
                 + pltpu.roll(a, LANES - ROT_HALF, 1) * sin_hi)
            if scale != 1.0:
                r = r * scale
            out_ref[:, p * LANES:(p + 1) * LANES] = r.astype(out_ref.dtype)

    rope_store(seg(0), qd_ref, DA_SCALE * LOG2_E)
    rope_store(seg(1), kd_ref, 1.0)
    vd_ref[...] = seg(2).astype(BF16)

    u = seg(3)
    qh_ref[...] = (u * _sigmoid(u) * HG_SCALE).astype(BF16)

    def lower_bound(d):
        l0 = lbl_ref[2 * d:2 * d + 1, :]
        l1 = lbl_ref[2 * d + 1:2 * d + 2, :]
        m = jnp.maximum(l0, l1)
        e0 = jnp.exp(l0 - m)
        e1 = jnp.exp(l1 - m)
        return e0 / (e0 + e1)

    lb = lower_bound(0)
    ff_ref[...] = lb + (1.0 - lb) * _sigmoid(seg(4))
    lb = lower_bound(1)
    fb_ref[...] = lb + (1.0 - lb) * _sigmoid(seg(5))
    ih_ref[...] = seg(6).astype(BF16)
    u = seg(7)
    gh_ref[...] = (u * _sigmoid(u)).astype(BF16)
    for p in range(D_MODEL // SEG):
        sgh_ref[:, p * SEG:(p + 1) * SEG] = _sigmoid(seg(8 + p)).astype(BF16)
        sgd_ref[:, p * SEG:(p + 1) * SEG] = _sigmoid(seg(10 + p)).astype(BF16)


def _inproj(x2, n1, w_in, lbl, rot, seq):
    n = x2.shape[0]
    tm = min(IN_ROWS, seq)
    nt = seq // tm
    row = lambda w: pl.BlockSpec((tm, w), lambda i: (i, 0))
    out_w = (DA_WIDTH,) * 3 + (HG_WIDTH,) * 5 + (D_MODEL,) * 2
    out_dt = (BF16, BF16, BF16, BF16, F32, F32, BF16, BF16, BF16, BF16)
    return pl.pallas_call(
        _inproj_kernel,
        grid=(n // tm,),
        in_specs=[row(D_MODEL), _const_spec((1, D_MODEL)), _const_spec((D_MODEL, IN_WIDTH)),
                  _const_spec((4, HG_WIDTH)),
                  pl.BlockSpec((tm, 3 * LANES), lambda i: (i % nt, 0))],
        out_specs=[row(w) for w in out_w],
        out_shape=[jax.ShapeDtypeStruct((n, w), dt) for w, dt in zip(out_w, out_dt)],
        compiler_params=pltpu.CompilerParams(dimension_semantics=("parallel",),
                                             vmem_limit_bytes=VMEM_LIMIT),
        name="inproj",
    )(x2, n1, w_in, lbl, rot)


def _attn_kernel(lam_ref, sub_ref, q_ref, k_ref, v_ref, o_ref, s_ref, p_ref):
    seq = k_ref.shape[0]
    nk = seq // ATT_K
    tiles = ATT_K // LANES
    lv = lam_ref[...]
    lam = (jnp.exp(jnp.sum(lv[0:1] * lv[1:2], axis=-1, keepdims=True))
           - jnp.exp(jnp.sum(lv[2:3] * lv[3:4], axis=-1, keepdims=True)) + LAM_INIT)
    lane = lax.broadcasted_iota(jnp.int32, (ATT_Q, DA_V_DIM), 1)

    def masked_q(x):
        q = q_ref[x * ATT_Q:(x + 1) * ATT_Q, :]
        zero = jnp.zeros_like(q)
        return jnp.concatenate([jnp.where(lane < DA_HEAD_DIM, q, zero),
                                jnp.where(lane >= DA_HEAD_DIM, q, zero)], axis=0)

    def scores(x, qz, j, mp):
        ks = slice(j * ATT_K, (j + 1) * ATT_K)
        s = lax.dot_general(qz, k_ref[ks, :], _NT, preferred_element_type=F32)
        s_ref[x, :, ks] = s
        for t in range(tiles):
            mp = jnp.maximum(mp, s[:, t * LANES:(t + 1) * LANES])
        return mp

    def exps(x, m, j, lp):
        ks = slice(j * ATT_K, (j + 1) * ATT_K)
        p = jnp.exp2(s_ref[x, :, ks] - m)
        for t in range(tiles):
            lp = lp + p[:, t * LANES:(t + 1) * LANES]
        p_ref[x, :, ks] = p.astype(BF16)
        return lp

    def mix(x, rho, j, acc):
        ks = slice(j * ATT_K, (j + 1) * ATT_K)
        w = p_ref[x, :ATT_Q, ks] - rho * p_ref[x, ATT_Q:, ks]
        return acc + jnp.dot(w, v_ref[ks, :], preferred_element_type=F32)

    qz = [None] * ATT_STREAMS
    mp = [None] * ATT_STREAMS
    lp = [None] * ATT_STREAMS
    m = [None] * ATT_STREAMS
    rho = [None] * ATT_STREAMS
    inv_l1 = [None] * ATT_STREAMS
    acc = [None] * ATT_STREAMS
    for t in range(ATT_STREAMS + 2):
        xs, xe, xm = t, t - 1, t - 2
        if xs < ATT_STREAMS:
            qz[xs] = masked_q(xs)
            mp[xs] = jnp.full((2 * ATT_Q, LANES), -jnp.inf, F32)
        if 0 <= xe < ATT_STREAMS:
            m[xe] = jnp.max(mp[xe], axis=-1, keepdims=True)
            lp[xe] = jnp.zeros((2 * ATT_Q, LANES), F32)
        if 0 <= xm < ATT_STREAMS:
            l = jnp.sum(lp[xm], axis=-1, keepdims=True)
            inv_l1[xm] = 1.0 / l[:ATT_Q]
            rho[xm] = (lam * l[:ATT_Q] / l[ATT_Q:]).astype(BF16)
            acc[xm] = jnp.zeros((ATT_Q, DA_V_DIM), F32)
        for j in range(nk):
            if xs < ATT_STREAMS:
                mp[xs] = scores(xs, qz[xs], j, mp[xs])
            if 0 <= xe < ATT_STREAMS:
                lp[xe] = exps(xe, m[xe], j, lp[xe])
            if 0 <= xm < ATT_STREAMS:
                acc[xm] = mix(xm, rho[xm], j, acc[xm])
        if 0 <= xm < ATT_STREAMS:
            o = acc[xm] * inv_l1[xm]
            ms = jnp.mean(o * o, axis=-1, keepdims=True)
            o = o * lax.rsqrt(ms + SUBLN_EPS) * sub_ref[...] * (1.0 - LAM_INIT)
            o_ref[xm * ATT_Q:(xm + 1) * ATT_Q, :] = o.astype(o_ref.dtype)


def _attention(lamv, subln, qd, kd, vd, batch, seq):
    n = qd.shape[0]
    tq = ATT_STREAMS * ATT_Q
    nq = seq // tq
    return pl.pallas_call(
        _attn_kernel,
        grid=(batch, DA_HEADS, nq),
        in_specs=[pl.BlockSpec((4, DA_HEAD_DIM), lambda b, h, i: (0, 0)),
                  pl.BlockSpec((1, DA_V_DIM), lambda b, h, i: (0, 0)),
                  pl.BlockSpec((tq, DA_V_DIM), lambda b, h, i: (b * nq + i, h)),
                  pl.BlockSpec((seq, DA_V_DIM), lambda b, h, i: (b, h)),
                  pl.BlockSpec((seq, DA_V_DIM), lambda b, h, i: (b, h))],
        out_specs=pl.BlockSpec((tq, DA_V_DIM), lambda b, h, i: (b * nq + i, h)),
        out_shape=jax.ShapeDtypeStruct((n, DA_WIDTH), BF16),
        scratch_shapes=[pltpu.VMEM((ATT_STREAMS, 2 * ATT_Q, seq), F32),
                        pltpu.VMEM((ATT_STREAMS, 2 * ATT_Q, seq), BF16)],
        compiler_params=pltpu.CompilerParams(
            dimension_semantics=("parallel", "parallel", "arbitrary"), vmem_limit_bytes=VMEM_LIMIT),
        name="diffattn",
    )(lamv, subln, qd, kd, vd)


def _cumsum_rows(x, rev):
    c = x.shape[0]
    row = lax.broadcasted_iota(jnp.int32, x.shape, 0)
    sh = 1
    while sh < c:
        if rev:
            x = x + jnp.where(row < c - sh, pltpu.roll(x, c - sh, 0), 0.0)
        else:
            x = x + jnp.where(row >= sh, pltpu.roll(x, sh, 0), 0.0)
        sh *= 2
    return x


def _block_row(g, half, rev):
    c, w = g.shape
    idx = half if rev else half - 1
    blk = 2 * half
    if blk >= SUBLANES:
        g3 = g.reshape(c // blk, blk, w)
        return jnp.broadcast_to(g3[:, idx:idx + 1, :], g3.shape).reshape(c, w)
    g3 = g.reshape(c // SUBLANES, SUBLANES, w)
    sub = lax.broadcasted_iota(jnp.int32, g3.shape, 1)
    out = None
    for b in reversed(range(SUBLANES // blk)):
        piece = jnp.broadcast_to(g3[:, b * blk + idx:b * blk + idx + 1, :], g3.shape)
        out = piece if out is None else jnp.where(sub < (b + 1) * blk, piece, out)
    return out.reshape(c, w)


def _gla_chunk(q, f, v, st_ref, pair_level, rev):
    c = q.shape[0]
    qf = q.astype(F32)
    vf = v.astype(F32)
    kk = 1.0 - f
    g = _cumsum_rows(jnp.log2(f), rev)
    row = lax.broadcasted_iota(jnp.int32, (c, 1), 0)

    st = st_ref[...]
    o = lax.dot_general((qf * jnp.exp2(g)).astype(BF16), st.astype(BF16), _NT,
                        preferred_element_type=F32)
    o = o + jnp.sum(qf * kk, axis=-1, keepdims=True) * vf

    a = jnp.zeros((c, c), F32)
    half = 1
    lvl = 0
    while half < c:
        d = jnp.exp2(-jnp.abs(g - _block_row(g, half, rev)))
        second = (row & half) != 0
        tgt = jnp.logical_not(second) if rev else second
        x = (jnp.where(tgt, qf, kk) * d).astype(BF16)
        al = lax.dot_general(x, x, _NT, preferred_element_type=F32)
        a = jnp.where(pair_level == lvl, al, a)
        half *= 2
        lvl += 1
    o = o + jnp.dot(a.astype(BF16), v, preferred_element_type=F32)

    g_tot = g[0:1, :] if rev else g[c - 1:c, :]
    ks = (kk * jnp.exp2(g_tot - g)).astype(BF16)
    st_ref[...] = st * jnp.exp2(g_tot) + lax.dot_general(v, ks, _TN, preferred_element_type=F32)
    return o


def _hgrn_kernel(lv_ref, qf_ref, ff_ref, vf_ref, qb_ref, fb_ref, vb_ref, of_ref, ob_ref, st_ref):
    @pl.when(pl.program_id(1) == 0)
    def _():
        st_ref[...] = jnp.zeros_like(st_ref)

    lv_fwd = lv_ref[0]
    lv_bwd = lv_ref[1]
    for h in range(HG_HEADS):
        sl = slice(h * HG_HEAD_DIM, (h + 1) * HG_HEAD_DIM)
        of_ref[:, sl] = _gla_chunk(qf_ref[:, sl], ff_ref[:, sl], vf_ref[:, sl], st_ref.at[h], lv_fwd, False)
        ob_ref[:, sl] = _gla_chunk(qb_ref[:, sl], fb_ref[:, sl], vb_ref[:, sl], st_ref.at[HG_HEADS + h],
                                   lv_bwd, True)


def _pair_levels(c):
    t = jnp.arange(c, dtype=jnp.int32)[:, None]
    s = jnp.arange(c, dtype=jnp.int32)[None, :]
    x = t ^ s
    lv = jnp.full((c, c), -1, jnp.int32)
    for i in range(c.bit_length() - 1):
        lv = jnp.where((x >> i) == 1, i, lv)
    return jnp.stack([jnp.where(t > s, lv, -1), jnp.where(t < s, lv, -1)])


def _hgrn(qh, ff, fb, ih, batch, seq):
    n = qh.shape[0]
    c = HG_CHUNK
    nc = seq // c
    fwd = pl.BlockSpec((c, HG_WIDTH), lambda b, j: (b * nc + j, 0))
    bwd = pl.BlockSpec((c, HG_WIDTH), lambda b, j: (b * nc + nc - 1 - j, 0))
    return pl.pallas_call(
        _hgrn_kernel,
        grid=(batch, nc),
        in_specs=[pl.BlockSpec((2, c, c), lambda b, j: (0, 0, 0)), fwd, fwd, fwd, bwd, bwd, bwd],
        out_specs=[fwd, bwd],
        out_shape=[jax.ShapeDtypeStruct((n, HG_WIDTH), F32)] * 2,
        scratch_shapes=[pltpu.VMEM((2 * HG_HEADS, HG_HEAD_DIM, HG_HEAD_DIM), F32)],
        compiler_params=pltpu.CompilerParams(dimension_semantics=("parallel", "arbitrary"),
                                             vmem_limit_bytes=VMEM_LIMIT),
        name="hgrn2",
    )(_pair_levels(c), qh, ff, ih, qh, fb, ih)


def _merge_mlp_kernel(x_ref, of_ref, ob_ref, gh_ref, oa_ref, sgh_ref, sgd_ref,
                      hgn_ref, n2_ref, fn_ref, whg_ref, wda_ref, wout_ref, w1_ref, w2_ref, y_ref):
    o = of_ref[...] + ob_ref[...]
    parts = []
    for h in range(HG_HEADS):
        sl = slice(h * HG_HEAD_DIM, (h + 1) * HG_HEAD_DIM)
        oh = o[:, sl]
        ms = jnp.mean(oh * oh, axis=-1, keepdims=True)
        parts.append(oh * lax.rsqrt(ms + NORM_EPS) * hgn_ref[:, sl])
    o = jnp.concatenate(parts, axis=1) * gh_ref[...].astype(F32)
    y_hg = jnp.dot(o.astype(BF16), whg_ref[...], preferred_element_type=F32)
    y_da = jnp.dot(oa_ref[...], wda_ref[...], preferred_element_type=F32)
    m = sgh_ref[...].astype(F32) * y_hg + sgd_ref[...].astype(F32) * y_da
    x1 = x_ref[...] + jnp.dot(m.astype(BF16), wout_ref[...], preferred_element_type=F32)

    ms = jnp.mean(x1 * x1, axis=-1, keepdims=True)
    h2 = (x1 * lax.rsqrt(ms + NORM_EPS) * n2_ref[...]).astype(BF16)
    acc = x1
    for j in range(D_FF // MLP_COLS):
        cs = slice(j * MLP_COLS, (j + 1) * MLP_COLS)
        mid = jnp.maximum(jnp.dot(h2, w1_ref[:, cs], preferred_element_type=F32), 0.0)
        acc = acc + jnp.dot((mid * mid).astype(BF16), w2_ref[cs, :], preferred_element_type=F32)
    ms = jnp.mean(acc * acc, axis=-1, keepdims=True)
    y_ref[...] = acc * lax.rsqrt(ms + NORM_EPS) * fn_ref[...]


def _merge_mlp(x2, of, ob, gh, oa, sgh, sgd, hgn, n2, fn, whg, wda, wout, w1, w2):
    n = x2.shape[0]
    tm = MLP_ROWS
    row = lambda w: pl.BlockSpec((tm, w), lambda i: (i, 0))
    return pl.pallas_call(
        _merge_mlp_kernel,
        grid=(n // tm,),
        in_specs=[row(D_MODEL), row(HG_WIDTH), row(HG_WIDTH), row(HG_WIDTH), row(DA_WIDTH),
                  row(D_MODEL), row(D_MODEL),
                  _const_spec((1, HG_WIDTH)), _const_spec((1, D_MODEL)), _const_spec((1, D_MODEL)),
                  _const_spec((HG_WIDTH, D_MODEL)), _const_spec((DA_WIDTH, D_MODEL)),
                  _const_spec((D_MODEL, D_MODEL)), _const_spec((D_MODEL, D_FF)),
                  _const_spec((D_FF, D_MODEL))],
        out_specs=row(D_MODEL),
        out_shape=jax.ShapeDtypeStruct((n, D_MODEL), F32),
        compiler_params=pltpu.CompilerParams(dimension_semantics=("parallel",),
                                             vmem_limit_bytes=VMEM_LIMIT),
        name="merge_mlp",
    )(x2, of, ob, gh, oa, sgh, sgd, hgn, n2, fn, whg, wda, wout, w1, w2)


def _rotary_table(seq):
    pos = jnp.arange(seq, dtype=F32)
    inv_freq = ROPE_THETA ** (-jnp.arange(0, ROT_DIM, 2, dtype=F32) / ROT_DIM)
    ang = pos[:, None] * inv_freq[None, :]
    cos, sin = jnp.cos(ang), jnp.sin(ang)
    pad = jnp.zeros((seq, DA_HEAD_DIM - ROT_DIM), F32)
    zeros = jnp.zeros_like(sin)
    cos64 = jnp.concatenate([cos, cos, pad + 1.0], axis=1)
    lo64 = jnp.concatenate([zeros, sin, pad], axis=1)
    hi64 = jnp.concatenate([-sin, zeros, pad], axis=1)
    return jnp.concatenate([cos64, cos64, lo64, lo64, hi64, hi64], axis=1)


def _trunk(x, p):
    batch, seq, _ = x.shape
    assert seq % max(IN_ROWS, ATT_STREAMS * ATT_Q, ATT_K, HG_CHUNK, MLP_ROWS) == 0
    x2 = x.reshape(batch * seq, D_MODEL)
    qd, kd, vd, qh, ff, fb, ih, gh, sgh, sgd = _inproj(x2, p["n1"], p["w_in"], p["lbl"],
                                                       _rotary_table(seq), seq)
    oa = _attention(p["lamv"], p["subln"], qd, kd, vd, batch, seq)
    of, ob = _hgrn(qh, ff, fb, ih, batch, seq)
    y = _merge_mlp(x2, of, ob, gh, oa, sgh, sgd, p["hgn"], p["n2"], p["fn"],
                   p["whg"], p["wda"], p["wout"], p["w1"], p["w2"])
    return y.reshape(batch, seq, D_MODEL)


def kernel(x_prompt, x_sample, norm1, w_in, hg_lb_logits, hg_norm, w_hg_branch, da_lambda_q1, da_lambda_k1, da_lambda_q2, da_lambda_k2, da_subln, w_da_branch, w_out, norm2, w_mlp_in, w_mlp_out, final_norm):
    assert norm1.shape[0] == 1 and hg_lb_logits.shape == (2, 2, HG_WIDTH)
    p = dict(
        n1=norm1.reshape(1, D_MODEL), w_in=w_in[0].astype(BF16),
        lbl=hg_lb_logits.reshape(4, HG_WIDTH),
        lamv=jnp.concatenate([da_lambda_q1, da_lambda_k1, da_lambda_q2, da_lambda_k2], axis=0),
        subln=da_subln.reshape(1, DA_V_DIM), hgn=hg_norm.reshape(1, HG_WIDTH),
        n2=norm2.reshape(1, D_MODEL), fn=final_norm.reshape(1, D_MODEL),
        whg=w_hg_branch[0].astype(BF16), wda=w_da_branch[0].astype(BF16), wout=w_out[0].astype(BF16),
        w1=w_mlp_in[0].astype(BF16), w2=w_mlp_out[0].astype(BF16),
    )
    return _trunk(x_prompt, p), _trunk(x_sample, p)
```

```python
import functools
import math

import jax
import jax.numpy as jnp
from jax import lax
from jax.experimental import pallas as pl
from jax.experimental.pallas import tpu as pltpu

F32 = jnp.float32
BF16 = jnp.bfloat16

D_MODEL = 1024
HG_HEADS = 4
HG_HEAD_DIM = 128
HG_WIDTH = HG_HEADS * HG_HEAD_DIM
HG_SCALE = HG_HEAD_DIM ** -0.5
DA_HEADS = 4
DA_HEAD_DIM = 64
DA_V_DIM = 2 * DA_HEAD_DIM
DA_WIDTH = DA_HEADS * DA_V_DIM
DA_SCALE = DA_HEAD_DIM ** -0.5
ROT_DIM = DA_HEAD_DIM // 4
ROT_HALF = ROT_DIM // 2
ROPE_THETA = 500000.0
D_FF = 4 * D_MODEL
NORM_EPS = 1e-6
SUBLN_EPS = 1e-5
IN_WIDTH = 3 * DA_WIDTH + 5 * HG_WIDTH + 2 * D_MODEL
LAM_INIT = 0.8 - 0.6 * math.exp(-0.3 * 0)

LANES = 128
SUBLANES = 8
VMEM_LIMIT = 56 * 1024 * 1024

SEG = 512
IN_ROWS = 512
ATT_Q = 128
ATT_STREAMS = 2
ATT_K = 512
LOG2_E = math.log2(math.e)
ATT_SAFE_LOG2 = 48.0
HG_CHUNK = 128
HG_ROWS = 2
MLP_ROWS = 256
MLP_COLS = 1024

_NT = (((1,), (1,)), ((), ()))
_TN = (((0,), (0,)), ((), ()))


def _sigmoid(x):
    return 1.0 / (1.0 + jnp.exp(-x))


def _const_spec(shape):
    nd = len(shape)
    return pl.BlockSpec(shape, lambda *_: (0,) * nd, pipeline_mode=pl.Buffered(1))


def _inproj_kernel(x_ref, n1_ref, w_ref, lbl_ref, rot_ref,
                   qd_ref, kd_ref, vd_ref, qh_ref, ff_ref, fb_ref, ih_ref, gh_ref, sgh_ref, sgd_ref):
    x = x_ref[...]
    ms = jnp.mean(x * x, axis=-1, keepdims=True)
    h = (x * lax.rsqrt(ms + NORM_EPS) * n1_ref[...]).astype(BF16)

    def seg(j):
        return jnp.dot(h, w_ref[:, j * SEG:(j + 1) * SEG], preferred_element_type=F32)

    cos_t = rot_ref[:, 0:LANES]
    sin_lo = rot_ref[:, LANES:2 * LANES]
    sin_hi = rot_ref[:, 2 * LANES:3 * LANES]

    def rope_store(u, out_ref, scale):
        for p in range(SEG // LANES):
            a = u[:, p * LANES:(p + 1) * LANES]
            r = (a * cos_t + pltpu.roll(a, ROT_HALF, 1) * sin_lo
                 + pltpu.roll(a, LANES - ROT_HALF, 1) * sin_hi)
            if scale != 1.0:
                r = r * scale
            out_ref[:, p * LANES:(p + 1) * LANES] = r.astype(out_ref.dtype)

    rope_store(seg(0), qd_ref, DA_SCALE * LOG2_E)
    rope_store(seg(1), kd_ref, 1.0)
    vd_ref[...] = seg(2).astype(BF16)

    u = seg(3)
    qh_ref[...] = (u * _sigmoid(u) * HG_SCALE).astype(BF16)

    def lower_bound(d):
        l0 = lbl_ref[2 * d:2 * d + 1, :]
        l1 = lbl_ref[2 * d + 1:2 * d + 2, :]
        m = jnp.maximum(l0, l1)
        e0 = jnp.exp(l0 - m)
        e1 = jnp.exp(l1 - m)
        return e0 / (e0 + e1)

    lb = lower_bound(0)
    ff_ref[...] = lb + (1.0 - lb) * _sigmoid(seg(4))
    lb = lower_bound(1)
    fb_ref[...] = lb + (1.0 - lb) * _sigmoid(seg(5))
    ih_ref[...] = seg(6).astype(BF16)
    u = seg(7)
    gh_ref[...] = (u * _sigmoid(u)).astype(BF16)
    for p in range(D_MODEL // SEG):
        sgh_ref[:, p * SEG:(p + 1) * SEG] = _sigmoid(seg(8 + p)).astype(BF16)
        sgd_ref[:, p * SEG:(p + 1) * SEG] = _sigmoid(seg(10 + p)).astype(BF16)


def _inproj(x2, n1, w_in, lbl, rot, seq):
    n = x2.shape[0]
    tm = min(IN_ROWS, seq)
    nt = seq // tm
    row = lambda w: pl.BlockSpec((tm, w), lambda i: (i, 0))
    out_w = (DA_WIDTH,) * 3 + (HG_WIDTH,) * 5 + (D_MODEL,) * 2
    out_dt = (BF16, BF16, BF16, BF16, F32, F32, BF16, BF16, BF16, BF16)
    return pl.pallas_call(
        _inproj_kernel,
        grid=(n // tm,),
        in_specs=[row(D_MODEL), _const_spec((1, D_MODEL)), _const_spec((D_MODEL, IN_WIDTH)),
                  _const_spec((4, HG_WIDTH)),
                  pl.BlockSpec((tm, 3 * LANES), lambda i: (i % nt, 0))],
        out_specs=[row(w) for w in out_w],
        out_shape=[jax.ShapeDtypeStruct((n, w), dt) for w, dt in zip(out_w, out_dt)],
        compiler_params=pltpu.CompilerParams(dimension_semantics=("parallel",),
                                             vmem_limit_bytes=VMEM_LIMIT),
        name="inproj",
    )(x2, n1, w_in, lbl, rot)


def _attn_kernel(lam_ref, sub_ref, q_ref, k_ref, v_ref, o_ref, kn_ref, s_ref, p_ref):
    seq = k_ref.shape[0]
    nk = seq // ATT_K
    tiles = ATT_K // LANES

    @pl.when(pl.program_id(2) == 0)
    def _():
        k32 = k_ref[...].astype(F32)
        d = lax.broadcasted_iota(jnp.int32, (DA_V_DIM, LANES), 0)
        c = lax.broadcasted_iota(jnp.int32, (DA_V_DIM, LANES), 1)
        pick = jnp.where(c == d // DA_HEAD_DIM, 1.0, 0.0).astype(BF16)
        n2 = jnp.dot((k32 * k32).astype(BF16), pick, preferred_element_type=F32)
        kn_ref[...] = jnp.max(n2, axis=0, keepdims=True)

    lv = lam_ref[...]
    lam = (jnp.exp(jnp.sum(lv[0:1] * lv[1:2], axis=-1, keepdims=True))
           - jnp.exp(jnp.sum(lv[2:3] * lv[3:4], axis=-1, keepdims=True)) + LAM_INIT)
    lane = lax.broadcasted_iota(jnp.int32, (ATT_Q, DA_V_DIM), 1)

    def masked_q(x):
        q = q_ref[x * ATT_Q:(x + 1) * ATT_Q, :]
        zero = jnp.zeros_like(q)
        return jnp.concatenate([jnp.where(lane < DA_HEAD_DIM, q, zero),
                                jnp.where(lane >= DA_HEAD_DIM, q, zero)], axis=0)

    def scores(x, qz, j, mp):
        ks = slice(j * ATT_K, (j + 1) * ATT_K)
        s = lax.dot_general(qz, k_ref[ks, :], _NT, preferred_element_type=F32)
        s_ref[x, :, ks] = s
        for t in range(tiles):
            mp = jnp.maximum(mp, s[:, t * LANES:(t + 1) * LANES])
        return mp

    def exps(x, m, j, lp):
        ks = slice(j * ATT_K, (j + 1) * ATT_K)
        p = jnp.exp2(s_ref[x, :, ks] - m)
        for t in range(tiles):
            lp = lp + p[:, t * LANES:(t + 1) * LANES]
        p_ref[x, :, ks] = p.astype(BF16)
        return lp

    def fused(x, qz, j, lp):
        ks = slice(j * ATT_K, (j + 1) * ATT_K)
        p = jnp.exp2(lax.dot_general(qz, k_ref[ks, :], _NT, preferred_element_type=F32))
        for t in range(tiles):
            lp = lp + p[:, t * LANES:(t + 1) * LANES]
        p_ref[x, :, ks] = p.astype(BF16)
        return lp

    def mix(x, rho, j, acc):
        ks = slice(j * ATT_K, (j + 1) * ATT_K)
        w = p_ref[x, :ATT_Q, ks] - rho * p_ref[x, ATT_Q:, ks]
        return acc + jnp.dot(w, v_ref[ks, :], preferred_element_type=F32)

    def finish(x, acc, lp):
        l = jnp.sum(lp, axis=-1, keepdims=True)
        if acc is None:
            return (lam * l[:ATT_Q] / l[ATT_Q:]).astype(BF16)
        o = acc * (1.0 / l[:ATT_Q])
        ms = jnp.mean(o * o, axis=-1, keepdims=True)
        o = o * lax.rsqrt(ms + SUBLN_EPS) * sub_ref[...] * (1.0 - LAM_INIT)
        o_ref[x * ATT_Q:(x + 1) * ATT_Q, :] = o.astype(o_ref.dtype)

    qz = [masked_q(x) for x in range(ATT_STREAMS)]
    kn2 = kn_ref[...]
    bound2 = jnp.zeros((1, 1), F32)
    for x in range(ATT_STREAMS):
        q32 = qz[x].astype(F32)
        qn2 = jnp.sum(q32 * q32, axis=-1, keepdims=True)
        bound2 = jnp.maximum(bound2, jnp.max(qn2[:ATT_Q], axis=0, keepdims=True) * kn2[:, 0:1])
        bound2 = jnp.maximum(bound2, jnp.max(qn2[ATT_Q:], axis=0, keepdims=True) * kn2[:, 1:2])
    safe = bound2[0, 0] <= 0.98 * ATT_SAFE_LOG2 * ATT_SAFE_LOG2

    @pl.when(safe)
    def _():
        for x in range(ATT_STREAMS):
            lp = jnp.zeros((2 * ATT_Q, LANES), F32)
            for j in range(nk):
                lp = fused(x, qz[x], j, lp)
            rho = finish(x, None, lp)
            acc = jnp.zeros((ATT_Q, DA_V_DIM), F32)
            for j in range(nk):
                acc = mix(x, rho, j, acc)
            finish(x, acc, lp)

    @pl.when(jnp.logical_not(safe))
    def _():
        mp = [None] * ATT_STREAMS
        lp = [None] * ATT_STREAMS
        m = [None] * ATT_STREAMS
        rho = [None] * ATT_STREAMS
        acc = [None] * ATT_STREAMS
        for t in range(ATT_STREAMS + 2):
            xs, xe, xm = t, t - 1, t - 2
            if xs < ATT_STREAMS:
                mp[xs] = jnp.full((2 * ATT_Q, LANES), -jnp.inf, F32)
            if 0 <= xe < ATT_STREAMS:
                m[xe] = jnp.max(mp[xe], axis=-1, keepdims=True)
                lp[xe] = jnp.zeros((2 * ATT_Q, LANES), F32)
            if 0 <= xm < ATT_STREAMS:
                rho[xm] = finish(xm, None, lp[xm])
                acc[xm] = jnp.zeros((ATT_Q, DA_V_DIM), F32)
            for j in range(nk):
                if xs < ATT_STREAMS:
                    mp[xs] = scores(xs, qz[xs], j, mp[xs])
                if 0 <= xe < ATT_STREAMS:
                    lp[xe] = exps(xe, m[xe], j, lp[xe])
                if 0 <= xm < ATT_STREAMS:
                    acc[xm] = mix(xm, rho[xm], j, acc[xm])
            if 0 <= xm < ATT_STREAMS:
                finish(xm, acc[xm], lp[xm])


def _attention(lamv, subln, qd, kd, vd, batch, seq):
    n = qd.shape[0]
    tq = ATT_STREAMS * ATT_Q
    nq = seq // tq
    return pl.pallas_call(
        _attn_kernel,
        grid=(batch, DA_HEADS, nq),
        in_specs=[pl.BlockSpec((4, DA_HEAD_DIM), lambda b, h, i: (0, 0)),
                  pl.BlockSpec((1, DA_V_DIM), lambda b, h, i: (0, 0)),
                  pl.BlockSpec((tq, DA_V_DIM), lambda b, h, i: (b * nq + i, h)),
                  pl.BlockSpec((seq, DA_V_DIM), lambda b, h, i: (b, h)),
                  pl.BlockSpec((seq, DA_V_DIM), lambda b, h, i: (b, h))],
        out_specs=pl.BlockSpec((tq, DA_V_DIM), lambda b, h, i: (b * nq + i, h)),
        out_shape=jax.ShapeDtypeStruct((n, DA_WIDTH), BF16),
        scratch_shapes=[pltpu.VMEM((1, LANES), F32),
                        pltpu.VMEM((ATT_STREAMS, 2 * ATT_Q, seq), F32),
                        pltpu.VMEM((ATT_STREAMS, 2 * ATT_Q, seq), BF16)],
        compiler_params=pltpu.CompilerParams(
            dimension_semantics=("parallel", "parallel", "arbitrary"), vmem_limit_bytes=VMEM_LIMIT),
        name="diffattn",
    )(lamv, subln, qd, kd, vd)


def _cumsum_rows(x, rev):
    c = x.shape[0]
    row = lax.broadcasted_iota(jnp.int32, x.shape, 0)
    sh = 1
    while sh < c:
        if rev:
            x = x + jnp.where(row < c - sh, pltpu.roll(x, c - sh, 0), 0.0)
        else:
            x = x + jnp.where(row >= sh, pltpu.roll(x, sh, 0), 0.0)
        sh *= 2
    return x


def _block_row(g, half, rev):
    c, w = g.shape
    idx = half if rev else half - 1
    blk = 2 * half
    if blk >= SUBLANES:
        g3 = g.reshape(c // blk, blk, w)
        return jnp.broadcast_to(g3[:, idx:idx + 1, :], g3.shape).reshape(c, w)
    g3 = g.reshape(c // SUBLANES, SUBLANES, w)
    sub = lax.broadcasted_iota(jnp.int32, g3.shape, 1)
    out = None
    for b in reversed(range(SUBLANES // blk)):
        piece = jnp.broadcast_to(g3[:, b * blk + idx:b * blk + idx + 1, :], g3.shape)
        out = piece if out is None else jnp.where(sub < (b + 1) * blk, piece, out)
    return out.reshape(c, w)


def _gla_chunk(q, f, v, st_ref, pair_level, rev):
    c = q.shape[0]
    qf = q.astype(F32)
    vf = v.astype(F32)
    kk = 1.0 - f
    g = _cumsum_rows(jnp.log2(f), rev)
    row = lax.broadcasted_iota(jnp.int32, (c, 1), 0)

    st = st_ref[...]
    o = lax.dot_general((qf * jnp.exp2(g)).astype(BF16), st.astype(BF16), _NT,
                        preferred_element_type=F32)
    o = o + jnp.sum(qf * kk, axis=-1, keepdims=True) * vf

    a = jnp.zeros((c, c), F32)
    half = 1
    lvl = 0
    while half < c:
        second = (row & half) != 0
        tgt = jnp.logical_not(second) if rev else second
        if half == 1:
            d = jnp.where(tgt, f, 1.0)
        else:
            d = jnp.exp2(-jnp.abs(g - _block_row(g, half, rev)))
        x = (jnp.where(tgt, qf, kk) * d).astype(BF16)
        al = lax.dot_general(x, x, _NT, preferred_element_type=F32)
        a = jnp.where(pair_level == lvl, al, a)
        half *= 2
        lvl += 1
    o = o + jnp.dot(a.astype(BF16), v, preferred_element_type=F32)

    g_tot = g[0:1, :] if rev else g[c - 1:c, :]
    ks = (kk * jnp.exp2(g_tot - g)).astype(BF16)
    st_ref[...] = st * jnp.exp2(g_tot) + lax.dot_general(v, ks, _TN, preferred_element_type=F32)
    return o


def _hgrn_kernel(lv_ref, qf_ref, ff_ref, vf_ref, qb_ref, fb_ref, vb_ref, of_ref, ob_ref, st_ref):
    @pl.when(pl.program_id(1) == 0)
    def _():
        st_ref[...] = jnp.zeros_like(st_ref)

    lv_fwd = lv_ref[0]
    lv_bwd = lv_ref[1]
    for r in range(HG_ROWS):
        for h in range(HG_HEADS):
            sl = slice(h * HG_HEAD_DIM, (h + 1) * HG_HEAD_DIM)
            of_ref[r, :, sl] = _gla_chunk(qf_ref[r, :, sl], ff_ref[r, :, sl], vf_ref[r, :, sl],
                                          st_ref.at[r, h], lv_fwd, False)
            ob_ref[r, :, sl] = _gla_chunk(qb_ref[r, :, sl], fb_ref[r, :, sl], vb_ref[r, :, sl],
                                          st_ref.at[r, HG_HEADS + h], lv_bwd, True)


def _pair_levels(c):
    t = jnp.arange(c, dtype=jnp.int32)[:, None]
    s = jnp.arange(c, dtype=jnp.int32)[None, :]
    x = t ^ s
    lv = jnp.full((c, c), -1, jnp.int32)
    for i in range(c.bit_length() - 1):
        lv = jnp.where((x >> i) == 1, i, lv)
    return jnp.stack([jnp.where(t > s, lv, -1), jnp.where(t < s, lv, -1)])


def _hgrn(qh, ff, fb, ih, batch, seq):
    c = HG_CHUNK
    assert batch % HG_ROWS == 0
    nc = seq // c
    fwd = pl.BlockSpec((HG_ROWS, c, HG_WIDTH), lambda b, j: (b, j, 0))
    bwd = pl.BlockSpec((HG_ROWS, c, HG_WIDTH), lambda b, j: (b, nc - 1 - j, 0))
    qh, ff, fb, ih = (a.reshape(batch, seq, HG_WIDTH) for a in (qh, ff, fb, ih))
    return pl.pallas_call(
        _hgrn_kernel,
        grid=(batch // HG_ROWS, nc),
        in_specs=[pl.BlockSpec((2, c, c), lambda b, j: (0, 0, 0)), fwd, fwd, fwd, bwd, bwd, bwd],
        out_specs=[fwd, bwd],
        out_shape=[jax.ShapeDtypeStruct((batch, seq, HG_WIDTH), F32)] * 2,
        scratch_shapes=[pltpu.VMEM((HG_ROWS, 2 * HG_HEADS, HG_HEAD_DIM, HG_HEAD_DIM), F32)],
        compiler_params=pltpu.CompilerParams(dimension_semantics=("parallel", "arbitrary"),
                                             vmem_limit_bytes=VMEM_LIMIT),
        name="hgrn2",
    )(_pair_levels(c), qh, ff, ih, qh, fb, ih)


def _merge_mlp_kernel(x_ref, of_ref, ob_ref, gh_ref, oa_ref, sgh_ref, sgd_ref,
                      hgn_ref, n2_ref, fn_ref, whg_ref, wda_ref, wout_ref, w1_ref, w2_ref, y_ref):
    o = of_ref[...] + ob_ref[...]
    parts = []
    for h in range(HG_HEADS):
        sl = slice(h * HG_HEAD_DIM, (h + 1) * HG_HEAD_DIM)
        oh = o[:, sl]
        ms = jnp.mean(oh * oh, axis=-1, keepdims=True)
        parts.append(oh * lax.rsqrt(ms + NORM_EPS) * hgn_ref[:, sl])
    o = jnp.concatenate(parts, axis=1) * gh_ref[...].astype(F32)
    y_hg = jnp.dot(o.astype(BF16), whg_ref[...], preferred_element_type=F32)
    y_da = jnp.dot(oa_ref[...], wda_ref[...], preferred_element_type=F32)
    m = sgh_ref[...].astype(F32) * y_hg + sgd_ref[...].astype(F32) * y_da
    x1 = x_ref[...] + jnp.dot(m.astype(BF16), wout_ref[...], preferred_element_type=F32)

    ms = jnp.mean(x1 * x1, axis=-1, keepdims=True)
    h2 = (x1 * lax.rsqrt(ms + NORM_EPS) * n2_ref[...]).astype(BF16)
    acc = x1
    for j in range(D_FF // MLP_COLS):
        cs = slice(j * MLP_COLS, (j + 1) * MLP_COLS)
        mid = jnp.maximum(jnp.dot(h2, w1_ref[:, cs], preferred_element_type=F32), 0.0)
        acc = acc + jnp.dot((mid * mid).astype(BF16), w2_ref[cs, :], preferred_element_type=F32)
    ms = jnp.mean(acc * acc, axis=-1, keepdims=True)
    y_ref[...] = acc * lax.rsqrt(ms + NORM_EPS) * fn_ref[...]


def _merge_mlp(x2, of, ob, gh, oa, sgh, sgd, hgn, n2, fn, whg, wda, wout, w1, w2):
    n = x2.shape[0]
    tm = MLP_ROWS
    row = lambda w: pl.BlockSpec((tm, w), lambda i: (i, 0))
    return pl.pallas_call(
        _merge_mlp_kernel,
        grid=(n // tm,),
        in_specs=[row(D_MODEL), row(HG_WIDTH), row(HG_WIDTH), row(HG_WIDTH), row(DA_WIDTH),
                  row(D_MODEL), row(D_MODEL),
                  _const_spec((1, HG_WIDTH)), _const_spec((1, D_MODEL)), _const_spec((1, D_MODEL)),
                  _const_spec((HG_WIDTH, D_MODEL)), _const_spec((DA_WIDTH, D_MODEL)),
                  _const_spec((D_MODEL, D_MODEL)), _const_spec((D_MODEL, D_FF)),
                  _const_spec((D_FF, D_MODEL))],
        out_specs=row(D_MODEL),
        out_shape=jax.ShapeDtypeStruct((n, D_MODEL), F32),
        compiler_params=pltpu.CompilerParams(dimension_semantics=("parallel",),
                                             vmem_limit_bytes=VMEM_LIMIT),
        name="merge_mlp",
    )(x2, of, ob, gh, oa, sgh, sgd, hgn, n2, fn, whg, wda, wout, w1, w2)


def _rotary_table(seq):
    pos = jnp.arange(seq, dtype=F32)
    inv_freq = ROPE_THETA ** (-jnp.arange(0, ROT_DIM, 2, dtype=F32) / ROT_DIM)
    ang = pos[:, None] * inv_freq[None, :]
    cos, sin = jnp.cos(ang), jnp.sin(ang)
    pad = jnp.zeros((seq, DA_HEAD_DIM - ROT_DIM), F32)
    zeros = jnp.zeros_like(sin)
    cos64 = jnp.concatenate([cos, cos, pad + 1.0], axis=1)
    lo64 = jnp.concatenate([zeros, sin, pad], axis=1)
    hi64 = jnp.concatenate([-sin, zeros, pad], axis=1)
    return jnp.concatenate([cos64, cos64, lo64, lo64, hi64, hi64], axis=1)


def _trunk(x, p):
    batch, seq, _ = x.shape
    assert seq % max(IN_ROWS, ATT_STREAMS * ATT_Q, ATT_K, HG_CHUNK, MLP_ROWS) == 0
    x2 = x.reshape(batch * seq, D_MODEL)
    qd, kd, vd, qh, ff, fb, ih, gh, sgh, sgd = _inproj(x2, p["n1"], p["w_in"], p["lbl"],
                                                       _rotary_table(seq), seq)
    oa = _attention(p["lamv"], p["subln"], qd, kd, vd, batch, seq)
    of, ob = (a.reshape(batch * seq, HG_WIDTH) for a in _hgrn(qh, ff, fb, ih, batch, seq))
    y = _merge_mlp(x2, of, ob, gh, oa, sgh, sgd, p["hgn"], p["n2"], p["fn"],
                   p["whg"], p["wda"], p["wout"], p["w1"], p["w2"])
    return y.reshape(batch, seq, D_MODEL)


def kernel(x_prompt, x_sample, norm1, w_in, hg_lb_logits, hg_norm, w_hg_branch, da_lambda_q1, da_lambda_k1, da_lambda_q2, da_lambda_k2, da_subln, w_da_branch, w_out, norm2, w_mlp_in, w_mlp_out, final_norm):
    assert norm1.shape[0] == 1 and hg_lb_logits.shape == (2, 2, HG_WIDTH)
    p = dict(
        n1=norm1.reshape(1, D_MODEL), w_in=w_in[0].astype(BF16),
        lbl=hg_lb_logits.reshape(4, HG_WIDTH),
        lamv=jnp.concatenate([da_lambda_q1, da_lambda_k1, da_lambda_q2, da_lambda_k2], axis=0),
        subln=da_subln.reshape(1, DA_V_DIM), hgn=hg_norm.reshape(1, HG_WIDTH),
        n2=norm2.reshape(1, D_MODEL), fn=final_norm.reshape(1, D_MODEL),
        whg=w_hg_branch[0].astype(BF16), wda=w_da_branch[0].astype(BF16), wout=w_out[0].astype(BF16),
        w1=w_mlp_in[0].astype(BF16), w2=w_mlp_out[0].astype(BF16),
    )
    return _trunk(x_prompt, p), _trunk(x_sample, p)
```

```python
import functools
import math

import jax
import jax.numpy as jnp
from jax import lax
from jax.experimental import pallas as pl
from jax.experimental.pallas import tpu as pltpu

F32 = jnp.float32
BF16 = jnp.bfloat16

D_MODEL = 1024
HG_HEADS = 4
HG_HEAD_DIM = 128
HG_WIDTH = HG_HEADS * HG_HEAD_DIM
HG_SCALE = HG_HEAD_DIM ** -0.5
DA_HEADS = 4
DA_HEAD_DIM = 64
DA_V_DIM = 2 * DA_HEAD_DIM
DA_WIDTH = DA_HEADS * DA_V_DIM
DA_SCALE = DA_HEAD_DIM ** -0.5
ROT_DIM = DA_HEAD_DIM // 4
ROT_HALF = ROT_DIM // 2
ROPE_THETA = 500000.0
D_FF = 4 * D_MODEL
NORM_EPS = 1e-6
SUBLN_EPS = 1e-5
IN_WIDTH = 3 * DA_WIDTH + 5 * HG_WIDTH + 2 * D_MODEL
LAM_INIT = 0.8 - 0.6 * math.exp(-0.3 * 0)

LANES = 128
SUBLANES = 8
VMEM_LIMIT = 56 * 1024 * 1024

SEG = 512
IN_ROWS = 512
ATT_Q = 128
ATT_STREAMS = 2
ATT_K = 512
LOG2_E = math.log2(math.e)
ATT_SAFE_LOG2 = 48.0
HG_CHUNK = 128
HG_ROWS = 2
MLP_ROWS = 256
MLP_COLS = 1024

_NT = (((1,), (1,)), ((), ()))
_TN = (((0,), (0,)), ((), ()))


def _sigmoid(x):
    return 1.0 / (1.0 + jnp.exp(-x))


def _const_spec(shape):
    nd = len(shape)
    return pl.BlockSpec(shape, lambda *_: (0,) * nd, pipeline_mode=pl.Buffered(1))


def _inproj_kernel(x_ref, n1_ref, w_ref, lbl_ref, rot_ref,
                   qd_ref, kd_ref, vd_ref, qh_ref, ff_ref, fb_ref, ih_ref, gh_ref, sgh_ref, sgd_ref):
    x = x_ref[...]
    ms = jnp.mean(x * x, axis=-1, keepdims=True)
    h = (x * lax.rsqrt(ms + NORM_EPS) * n1_ref[...]).astype(BF16)

    def seg(j):
        return jnp.dot(h, w_ref[:, j * SEG:(j + 1) * SEG], preferred_element_type=F32)

    cos_t = rot_ref[:, 0:LANES]
    sin_lo = rot_ref[:, LANES:2 * LANES]
    sin_hi = rot_ref[:, 2 * LANES:3 * LANES]

    def rope_store(u, out_ref, scale):
        for p in range(SEG // LANES):
            a = u[:, p * LANES:(p + 1) * LANES]
            r = (a * cos_t + pltpu.roll(a, ROT_HALF, 1) * sin_lo
                 + pltpu.roll(a, LANES - ROT_HALF, 1) * sin_hi)
            if scale != 1.0:
                r = r * scale
            out_ref[:, p * LANES:(p + 1) * LANES] = r.astype(out_ref.dtype)

    rope_store(seg(0), qd_ref, DA_SCALE * LOG2_E)
    rope_store(seg(1), kd_ref, 1.0)
    vd_ref[...] = seg(2).astype(BF16)

    u = seg(3)
    qh_ref[...] = (u * _sigmoid(u) * HG_SCALE).astype(BF16)

    def lower_bound(d):
        l0 = lbl_ref[2 * d:2 * d + 1, :]
        l1 = lbl_ref[2 * d + 1:2 * d + 2, :]
        m = jnp.maximum(l0, l1)
        e0 = jnp.exp(l0 - m)
        e1 = jnp.exp(l1 - m)
        return e0 / (e0 + e1)

    lb = lower_bound(0)
    ff_ref[...] = lb + (1.0 - lb) * _sigmoid(seg(4))
    lb = lower_bound(1)
    fb_ref[...] = lb + (1.0 - lb) * _sigmoid(seg(5))
    ih_ref[...] = seg(6).astype(BF16)
    u = seg(7)
    gh_ref[...] = (u * _sigmoid(u)).astype(BF16)
    for p in range(D_MODEL // SEG):
        sgh_ref[:, p * SEG:(p + 1) * SEG] = _sigmoid(seg(8 + p)).astype(BF16)
        sgd_ref[:, p * SEG:(p + 1) * SEG] = _sigmoid(seg(10 + p)).astype(BF16)


def _inproj(x2, n1, w_in, lbl, rot, seq):
    n = x2.shape[0]
    tm = min(IN_ROWS, seq)
    nt = seq // tm
    row = lambda w: pl.BlockSpec((tm, w), lambda i: (i, 0))
    out_w = (DA_WIDTH,) * 3 + (HG_WIDTH,) * 5 + (D_MODEL,) * 2
    out_dt = (BF16, BF16, BF16, BF16, F32, F32, BF16, BF16, BF16, BF16)
    return pl.pallas_call(
        _inproj_kernel,
        grid=(n // tm,),
        in_specs=[row(D_MODEL), _const_spec((1, D_MODEL)), _const_spec((D_MODEL, IN_WIDTH)),
                  _const_spec((4, HG_WIDTH)),
                  pl.BlockSpec((tm, 3 * LANES), lambda i: (i % nt, 0))],
        out_specs=[row(w) for w in out_w],
        out_shape=[jax.ShapeDtypeStruct((n, w), dt) for w, dt in zip(out_w, out_dt)],
        compiler_params=pltpu.CompilerParams(dimension_semantics=("parallel",),
                                             vmem_limit_bytes=VMEM_LIMIT),
        name="inproj",
    )(x2, n1, w_in, lbl, rot)


def _attn_kernel(lam_ref, sub_ref, q_ref, k_ref, v_ref, o_ref, safe_ref, s_ref, p_ref):
    seq = k_ref.shape[0]
    nk = seq // ATT_K
    tiles = ATT_K // LANES
    step = pl.program_id(2)

    @pl.when(step == 0)
    def _():
        d = lax.broadcasted_iota(jnp.int32, (DA_V_DIM, LANES), 0)
        c = lax.broadcasted_iota(jnp.int32, (DA_V_DIM, LANES), 1)
        pick = jnp.where(c == d // DA_HEAD_DIM, 1.0, 0.0).astype(BF16)

        def max_norm2(ref):
            a = ref[...].astype(F32)
            return jnp.max(jnp.dot((a * a).astype(BF16), pick, preferred_element_type=F32), axis=0, keepdims=True)

        b2 = max_norm2(q_ref) * max_norm2(k_ref)
        b2 = jnp.maximum(b2[:, 0:1], b2[:, 1:2])
        safe_ref[0] = (b2[0, 0] <= 0.98 * ATT_SAFE_LOG2 * ATT_SAFE_LOG2).astype(jnp.int32)

    lv = lam_ref[...]
    lam = (jnp.exp(jnp.sum(lv[0:1] * lv[1:2], axis=-1, keepdims=True))
           - jnp.exp(jnp.sum(lv[2:3] * lv[3:4], axis=-1, keepdims=True)) + LAM_INIT)
    lane = lax.broadcasted_iota(jnp.int32, (ATT_Q, DA_V_DIM), 1)

    def masked_q(x):
        q = q_ref[pl.ds(pl.multiple_of((step * ATT_STREAMS + x) * ATT_Q, ATT_Q), ATT_Q), :]
        zero = jnp.zeros_like(q)
        return jnp.concatenate([jnp.where(lane < DA_HEAD_DIM, q, zero),
                                jnp.where(lane >= DA_HEAD_DIM, q, zero)], axis=0)

    def scores(x, qz, j, mp):
        ks = slice(j * ATT_K, (j + 1) * ATT_K)
        s = lax.dot_general(qz, k_ref[ks, :], _NT, preferred_element_type=F32)
        s_ref[x, :, ks] = s
        for t in range(tiles):
            mp = jnp.maximum(mp, s[:, t * LANES:(t + 1) * LANES])
        return mp

    def exps(x, m, j, lp):
        ks = slice(j * ATT_K, (j + 1) * ATT_K)
        p = jnp.exp2(s_ref[x, :, ks] - m)
        for t in range(tiles):
            lp = lp + p[:, t * LANES:(t + 1) * LANES]
        p_ref[x, :, ks] = p.astype(BF16)
        return lp

    def fused(x, qz, j, lp):
        ks = slice(j * ATT_K, (j + 1) * ATT_K)
        p = jnp.exp2(lax.dot_general(qz, k_ref[ks, :], _NT, preferred_element_type=F32))
        for t in range(tiles):
            lp = lp + p[:, t * LANES:(t + 1) * LANES]
        p_ref[x, :, ks] = p.astype(BF16)
        return lp

    def mix(x, rho, j, acc):
        ks = slice(j * ATT_K, (j + 1) * ATT_K)
        w = p_ref[x, :ATT_Q, ks] - rho * p_ref[x, ATT_Q:, ks]
        return acc + jnp.dot(w, v_ref[ks, :], preferred_element_type=F32)

    def finish(x, acc, lp):
        l = jnp.sum(lp, axis=-1, keepdims=True)
        if acc is None:
            return (lam * l[:ATT_Q] / l[ATT_Q:]).astype(BF16)
        o = acc * (1.0 / l[:ATT_Q])
        ms = jnp.mean(o * o, axis=-1, keepdims=True)
        o = o * lax.rsqrt(ms + SUBLN_EPS) * sub_ref[...] * (1.0 - LAM_INIT)
        o_ref[x * ATT_Q:(x + 1) * ATT_Q, :] = o.astype(o_ref.dtype)

    safe = safe_ref[0] == 1

    @pl.when(safe)
    def _():
        lps = []
        for x in range(ATT_STREAMS):
            qz = masked_q(x)
            lp = jnp.zeros((2 * ATT_Q, LANES), F32)
            for j in range(nk):
                lp = fused(x, qz, j, lp)
            lps.append(lp)
        for x in range(ATT_STREAMS):
            rho = finish(x, None, lps[x])
            acc = jnp.zeros((ATT_Q, DA_V_DIM), F32)
            for j in range(nk):
                acc = mix(x, rho, j, acc)
            finish(x, acc, lps[x])

    @pl.when(jnp.logical_not(safe))
    def _():
        qz = [None] * ATT_STREAMS
        mp = [None] * ATT_STREAMS
        lp = [None] * ATT_STREAMS
        m = [None] * ATT_STREAMS
        rho = [None] * ATT_STREAMS
        acc = [None] * ATT_STREAMS
        for t in range(ATT_STREAMS + 2):
            xs, xe, xm = t, t - 1, t - 2
            if xs < ATT_STREAMS:
                qz[xs] = masked_q(xs)
                mp[xs] = jnp.full((2 * ATT_Q, LANES), -jnp.inf, F32)
            if 0 <= xe < ATT_STREAMS:
                m[xe] = jnp.max(mp[xe], axis=-1, keepdims=True)
                lp[xe] = jnp.zeros((2 * ATT_Q, LANES), F32)
            if 0 <= xm < ATT_STREAMS:
                rho[xm] = finish(xm, None, lp[xm])
                acc[xm] = jnp.zeros((ATT_Q, DA_V_DIM), F32)
            for j in range(nk):
                if xs < ATT_STREAMS:
                    mp[xs] = scores(xs, qz[xs], j, mp[xs])
                if 0 <= xe < ATT_STREAMS:
                    lp[xe] = exps(xe, m[xe], j, lp[xe])
                if 0 <= xm < ATT_STREAMS:
                    acc[xm] = mix(xm, rho[xm], j, acc[xm])
            if 0 <= xm < ATT_STREAMS:
                finish(xm, acc[xm], lp[xm])


def _attention(lamv, subln, qd, kd, vd, batch, seq):
    n = qd.shape[0]
    tq = ATT_STREAMS * ATT_Q
    nq = seq // tq
    assert nq * tq == seq
    slab = pl.BlockSpec((seq, DA_V_DIM), lambda b, h, i: (b, h))
    return pl.pallas_call(
        _attn_kernel,
        grid=(batch, DA_HEADS, nq),
        in_specs=[pl.BlockSpec((4, DA_HEAD_DIM), lambda b, h, i: (0, 0)),
                  pl.BlockSpec((1, DA_V_DIM), lambda b, h, i: (0, 0)),
                  slab, slab, slab],
        out_specs=pl.BlockSpec((tq, DA_V_DIM), lambda b, h, i: (b * nq + i, h)),
        out_shape=jax.ShapeDtypeStruct((n, DA_WIDTH), BF16),
        scratch_shapes=[pltpu.SMEM((1,), jnp.int32),
                        pltpu.VMEM((ATT_STREAMS, 2 * ATT_Q, seq), F32),
                        pltpu.VMEM((ATT_STREAMS, 2 * ATT_Q, seq), BF16)],
        compiler_params=pltpu.CompilerParams(
            dimension_semantics=("parallel", "parallel", "arbitrary"), vmem_limit_bytes=VMEM_LIMIT),
        name="diffattn",
    )(lamv, subln, qd, kd, vd)


def _cumsum_rows(x, rev):
    c = x.shape[0]
    row = lax.broadcasted_iota(jnp.int32, x.shape, 0)
    sh = 1
    while sh < c:
        if rev:
            x = x + jnp.where(row < c - sh, pltpu.roll(x, c - sh, 0), 0.0)
        else:
            x = x + jnp.where(row >= sh, pltpu.roll(x, sh, 0), 0.0)
        sh *= 2
    return x


def _block_row(g, half, rev):
    c, w = g.shape
    idx = half if rev else half - 1
    blk = 2 * half
    if blk >= SUBLANES:
        g3 = g.reshape(c // blk, blk, w)
        return jnp.broadcast_to(g3[:, idx:idx + 1, :], g3.shape).reshape(c, w)
    g3 = g.reshape(c // SUBLANES, SUBLANES, w)
    sub = lax.broadcasted_iota(jnp.int32, g3.shape, 1)
    out = None
    for b in reversed(range(SUBLANES // blk)):
        piece = jnp.broadcast_to(g3[:, b * blk + idx:b * blk + idx + 1, :], g3.shape)
        out = piece if out is None else jnp.where(sub < (b + 1) * blk, piece, out)
    return out.reshape(c, w)


def _gla_chunk(q, f, v, st_ref, pair_level, rev):
    c = q.shape[0]
    qf = q.astype(F32)
    vf = v.astype(F32)
    kk = 1.0 - f
    g = _cumsum_rows(jnp.log2(f), rev)
    row = lax.broadcasted_iota(jnp.int32, (c, 1), 0)

    st = st_ref[...]
    o = lax.dot_general((qf * jnp.exp2(g)).astype(BF16), st.astype(BF16), _NT,
                        preferred_element_type=F32)
    o = o + jnp.sum(qf * kk, axis=-1, keepdims=True) * vf

    a = jnp.zeros((c, c), F32)
    half = 1
    lvl = 0
    while half < c:
        second = (row & half) != 0
        tgt = jnp.logical_not(second) if rev else second
        if half == 1:
            d = jnp.where(tgt, f, 1.0)
        else:
            d = jnp.exp2((g - _block_row(g, half, rev)) * jnp.where(tgt, 1.0, -1.0))
        x = (jnp.where(tgt, qf, kk) * d).astype(BF16)
        al = lax.dot_general(x, x, _NT, preferred_element_type=F32)
        a = jnp.where(pair_level == lvl, al, a)
        half *= 2
        lvl += 1
    o = o + jnp.dot(a.astype(BF16), v, preferred_element_type=F32)

    g_tot = g[0:1, :] if rev else g[c - 1:c, :]
    ks = (kk * jnp.exp2(g_tot - g)).astype(BF16)
    st_ref[...] = st * jnp.exp2(g_tot) + lax.dot_general(v, ks, _TN, preferred_element_type=F32)
    return o


def _hgrn_kernel(lv_ref, qf_ref, ff_ref, vf_ref, qb_ref, fb_ref, vb_ref, of_ref, ob_ref, st_ref):
    @pl.when(pl.program_id(1) == 0)
    def _():
        st_ref[...] = jnp.zeros_like(st_ref)

    lv_fwd = lv_ref[0]
    lv_bwd = lv_ref[1]
    for r in range(HG_ROWS):
        for h in range(HG_HEADS):
            sl = slice(h * HG_HEAD_DIM, (h + 1) * HG_HEAD_DIM)
            of_ref[r, :, sl] = _gla_chunk(qf_ref[r, :, sl], ff_ref[r, :, sl], vf_ref[r, :, sl],
                                          st_ref.at[r, h], lv_fwd, False)
            ob_ref[r, :, sl] = _gla_chunk(qb_ref[r, :, sl], fb_ref[r, :, sl], vb_ref[r, :, sl],
                                          st_ref.at[r, HG_HEADS + h], lv_bwd, True)


def _pair_levels(c):
    t = jnp.arange(c, dtype=jnp.int32)[:, None]
    s = jnp.arange(c, dtype=jnp.int32)[None, :]
    x = t ^ s
    lv = jnp.full((c, c), -1, jnp.int32)
    for i in range(c.bit_length() - 1):
        lv = jnp.where((x >> i) == 1, i, lv)
    return jnp.stack([jnp.where(t > s, lv, -1), jnp.where(t < s, lv, -1)])


def _hgrn(qh, ff, fb, ih, batch, seq):
    c = HG_CHUNK
    assert batch % HG_ROWS == 0
    nc = seq // c
    fwd = pl.BlockSpec((HG_ROWS, c, HG_WIDTH), lambda b, j: (b, j, 0))
    bwd = pl.BlockSpec((HG_ROWS, c, HG_WIDTH), lambda b, j: (b, nc - 1 - j, 0))
    qh, ff, fb, ih = (a.reshape(batch, seq, HG_WIDTH) for a in (qh, ff, fb, ih))
    return pl.pallas_call(
        _hgrn_kernel,
        grid=(batch // HG_ROWS, nc),
        in_specs=[pl.BlockSpec((2, c, c), lambda b, j: (0, 0, 0)), fwd, fwd, fwd, bwd, bwd, bwd],
        out_specs=[fwd, bwd],
        out_shape=[jax.ShapeDtypeStruct((batch, seq, HG_WIDTH), F32)] * 2,
        scratch_shapes=[pltpu.VMEM((HG_ROWS, 2 * HG_HEADS, HG_HEAD_DIM, HG_HEAD_DIM), F32)],
        compiler_params=pltpu.CompilerParams(dimension_semantics=("parallel", "arbitrary"),
                                             vmem_limit_bytes=VMEM_LIMIT),
        name="hgrn2",
    )(_pair_levels(c), qh, ff, ih, qh, fb, ih)


def _merge_mlp_kernel(x_ref, of_ref, ob_ref, gh_ref, oa_ref, sgh_ref, sgd_ref,
                      hgn_ref, n2_ref, fn_ref, whg_ref, wda_ref, wout_ref, w1_ref, w2_ref, y_ref):
    o = of_ref[...] + ob_ref[...]
    parts = []
    for h in range(HG_HEADS):
        sl = slice(h * HG_HEAD_DIM, (h + 1) * HG_HEAD_DIM)
        oh = o[:, sl]
        ms = jnp.mean(oh * oh, axis=-1, keepdims=True)
        parts.append(oh * lax.rsqrt(ms + NORM_EPS) * hgn_ref[:, sl])
    o = jnp.concatenate(parts, axis=1) * gh_ref[...].astype(F32)
    y_hg = jnp.dot(o.astype(BF16), whg_ref[...], preferred_element_type=F32)
    y_da = jnp.dot(oa_ref[...], wda_ref[...], preferred_element_type=F32)
    m = sgh_ref[...].astype(F32) * y_hg + sgd_ref[...].astype(F32) * y_da
    x1 = x_ref[...] + jnp.dot(m.astype(BF16), wout_ref[...], preferred_element_type=F32)

    ms = jnp.mean(x1 * x1, axis=-1, keepdims=True)
    h2 = (x1 * lax.rsqrt(ms + NORM_EPS) * n2_ref[...]).astype(BF16)
    acc = x1
    for j in range(D_FF // MLP_COLS):
        cs = slice(j * MLP_COLS, (j + 1) * MLP_COLS)
        mid = jnp.maximum(jnp.dot(h2, w1_ref[:, cs], preferred_element_type=F32), 0.0)
        acc = acc + jnp.dot((mid * mid).astype(BF16), w2_ref[cs, :], preferred_element_type=F32)
    ms = jnp.mean(acc * acc, axis=-1, keepdims=True)
    y_ref[...] = acc * lax.rsqrt(ms + NORM_EPS) * fn_ref[...]


def _merge_mlp(x2, of, ob, gh, oa, sgh, sgd, hgn, n2, fn, whg, wda, wout, w1, w2):
    n = x2.shape[0]
    tm = MLP_ROWS
    row = lambda w: pl.BlockSpec((tm, w), lambda i: (i, 0))
    return pl.pallas_call(
        _merge_mlp_kernel,
        grid=(n // tm,),
        in_specs=[row(D_MODEL), row(HG_WIDTH), row(HG_WIDTH), row(HG_WIDTH), row(DA_WIDTH),
                  row(D_MODEL), row(D_MODEL),
                  _const_spec((1, HG_WIDTH)), _const_spec((1, D_MODEL)), _const_spec((1, D_MODEL)),
                  _const_spec((HG_WIDTH, D_MODEL)), _const_spec((DA_WIDTH, D_MODEL)),
                  _const_spec((D_MODEL, D_MODEL)), _const_spec((D_MODEL, D_FF)),
                  _const_spec((D_FF, D_MODEL))],
        out_specs=row(D_MODEL),
        out_shape=jax.ShapeDtypeStruct((n, D_MODEL), F32),
        compiler_params=pltpu.CompilerParams(dimension_semantics=("parallel",),
                                             vmem_limit_bytes=VMEM_LIMIT),
        name="merge_mlp",
    )(x2, of, ob, gh, oa, sgh, sgd, hgn, n2, fn, whg, wda, wout, w1, w2)


def _rotary_table(seq):
    pos = jnp.arange(seq, dtype=F32)
    inv_freq = ROPE_THETA ** (-jnp.arange(0, ROT_DIM, 2, dtype=F32) / ROT_DIM)
    ang = pos[:, None] * inv_freq[None, :]
    cos, sin = jnp.cos(ang), jnp.sin(ang)
    pad = jnp.zeros((seq, DA_HEAD_DIM - ROT_DIM), F32)
    zeros = jnp.zeros_like(sin)
    cos64 = jnp.concatenate([cos, cos, pad + 1.0], axis=1)
    lo64 = jnp.concatenate([zeros, sin, pad], axis=1)
    hi64 = jnp.concatenate([-sin, zeros, pad], axis=1)
    return jnp.concatenate([cos64, cos64, lo64, lo64, hi64, hi64], axis=1)


def _trunk(x, p):
    batch, seq, _ = x.shape
    assert seq % max(IN_ROWS, ATT_STREAMS * ATT_Q, ATT_K, HG_CHUNK, MLP_ROWS) == 0
    x2 = x.reshape(batch * seq, D_MODEL)
    qd, kd, vd, qh, ff, fb, ih, gh, sgh, sgd = _inproj(x2, p["n1"], p["w_in"], p["lbl"],
                                                       _rotary_table(seq), seq)
    oa = _attention(p["lamv"], p["subln"], qd, kd, vd, batch, seq)
    of, ob = (a.reshape(batch * seq, HG_WIDTH) for a in _hgrn(qh, ff, fb, ih, batch, seq))
    y = _merge_mlp(x2, of, ob, gh, oa, sgh, sgd, p["hgn"], p["n2"], p["fn"],
                   p["whg"], p["wda"], p["wout"], p["w1"], p["w2"])
    return y.reshape(batch, seq, D_MODEL)


def kernel(x_prompt, x_sample, norm1, w_in, hg_lb_logits, hg_norm, w_hg_branch, da_lambda_q1, da_lambda_k1, da_lambda_q2, da_lambda_k2, da_subln, w_da_branch, w_out, norm2, w_mlp_in, w_mlp_out, final_norm):
    assert norm1.shape[0] == 1 and hg_lb_logits.shape == (2, 2, HG_WIDTH)
    p = dict(
        n1=norm1.reshape(1, D_MODEL), w_in=w_in[0].astype(BF16),
        lbl=hg_lb_logits.reshape(4, HG_WIDTH),
        lamv=jnp.concatenate([da_lambda_q1, da_lambda_k1, da_lambda_q2, da_lambda_k2], axis=0),
        subln=da_subln.reshape(1, DA_V_DIM), hgn=hg_norm.reshape(1, HG_WIDTH),
        n2=norm2.reshape(1, D_MODEL), fn=final_norm.reshape(1, D_MODEL),
        whg=w_hg_branch[0].astype(BF16), wda=w_da_branch[0].astype(BF16), wout=w_out[0].astype(BF16),
        w1=w_mlp_in[0].astype(BF16), w2=w_mlp_out[0].astype(BF16),
    )
    return _trunk(x_prompt, p), _trunk(x_sample, p)
```

```python
import functools
import math

import jax
import jax.numpy as jnp
from jax import lax
from jax.experimental import pallas as pl
from jax.experimental.pallas import tpu as pltpu

F32 = jnp.float32
BF16 = jnp.bfloat16

D_MODEL = 1024
HG_HEADS = 4
HG_HEAD_DIM = 128
HG_WIDTH = HG_HEADS * HG_HEAD_DIM
HG_SCALE = HG_HEAD_DIM ** -0.5
DA_HEADS = 4
DA_HEAD_DIM = 64
DA_V_DIM = 2 * DA_HEAD_DIM
DA_WIDTH = DA_HEADS * DA_V_DIM
DA_SCALE = DA_HEAD_DIM ** -0.5
ROT_DIM = DA_HEAD_DIM // 4
ROT_HALF = ROT_DIM // 2
ROPE_THETA = 500000.0
D_FF = 4 * D_MODEL
NORM_EPS = 1e-6
SUBLN_EPS = 1e-5
IN_WIDTH = 3 * DA_WIDTH + 5 * HG_WIDTH + 2 * D_MODEL
LAM_INIT = 0.8 - 0.6 * math.exp(-0.3 * 0)

LANES = 128
SUBLANES = 8
VMEM_LIMIT = 56 * 1024 * 1024

SEG = 512
IN_ROWS = 1024
ATT_Q = 128
ATT_STREAMS = 2
ATT_K = 512
LOG2_E = math.log2(math.e)
ATT_SAFE_LOG2 = 48.0
HG_CHUNK = 128
HG_ROWS = 4
MLP_ROWS = 512
MLP_COLS = 1024

_NT = (((1,), (1,)), ((), ()))
_TN = (((0,), (0,)), ((), ()))


def _sigmoid(x):
    return 1.0 / (1.0 + jnp.exp(-x))


def _const_spec(shape):
    nd = len(shape)
    return pl.BlockSpec(shape, lambda *_: (0,) * nd, pipeline_mode=pl.Buffered(1))


def _inproj_kernel(x_ref, n1_ref, w_ref, lbl_ref, rot_ref,
                   qd_ref, kd_ref, vd_ref, qh_ref, ff_ref, fb_ref, ih_ref, gh_ref, sgh_ref, sgd_ref):
    x = x_ref[...]
    ms = jnp.mean(x * x, axis=-1, keepdims=True)
    h = (x * lax.rsqrt(ms + NORM_EPS) * n1_ref[...]).astype(BF16)

    def seg(j):
        return jnp.dot(h, w_ref[:, j * SEG:(j + 1) * SEG], preferred_element_type=F32)

    cos_t = rot_ref[:, 0:LANES]
    sin_lo = rot_ref[:, LANES:2 * LANES]
    sin_hi = rot_ref[:, 2 * LANES:3 * LANES]

    def rope_store(u, out_ref, scale):
        for p in range(SEG // LANES):
            a = u[:, p * LANES:(p + 1) * LANES]
            r = (a * cos_t + pltpu.roll(a, ROT_HALF, 1) * sin_lo
                 + pltpu.roll(a, LANES - ROT_HALF, 1) * sin_hi)
            if scale != 1.0:
                r = r * scale
            out_ref[:, p * LANES:(p + 1) * LANES] = r.astype(out_ref.dtype)

    rope_store(seg(0), qd_ref, DA_SCALE * LOG2_E)
    rope_store(seg(1), kd_ref, 1.0)
    vd_ref[...] = seg(2).astype(BF16)

    u = seg(3)
    qh_ref[...] = (u * _sigmoid(u) * HG_SCALE).astype(BF16)

    def lower_bound(d):
        l0 = lbl_ref[2 * d:2 * d + 1, :]
        l1 = lbl_ref[2 * d + 1:2 * d + 2, :]
        m = jnp.maximum(l0, l1)
        e0 = jnp.exp(l0 - m)
        e1 = jnp.exp(l1 - m)
        return e0 / (e0 + e1)

    lb = lower_bound(0)
    ff_ref[...] = lb + (1.0 - lb) * _sigmoid(seg(4))
    lb = lower_bound(1)
    fb_ref[...] = lb + (1.0 - lb) * _sigmoid(seg(5))
    ih_ref[...] = seg(6).astype(BF16)
    u = seg(7)
    gh_ref[...] = (u * _sigmoid(u)).astype(BF16)
    for p in range(D_MODEL // SEG):
        sgh_ref[:, p * SEG:(p + 1) * SEG] = _sigmoid(seg(8 + p)).astype(BF16)
        sgd_ref[:, p * SEG:(p + 1) * SEG] = _sigmoid(seg(10 + p)).astype(BF16)


def _inproj(x2, n1, w_in, lbl, rot, seq):
    n = x2.shape[0]
    tm = min(IN_ROWS, seq)
    nt = seq // tm
    row = lambda w: pl.BlockSpec((tm, w), lambda i: (i, 0))
    out_w = (DA_WIDTH,) * 3 + (HG_WIDTH,) * 5 + (D_MODEL,) * 2
    out_dt = (BF16, BF16, BF16, BF16, F32, F32, BF16, BF16, BF16, BF16)
    return pl.pallas_call(
        _inproj_kernel,
        grid=(n // tm,),
        in_specs=[row(D_MODEL), _const_spec((1, D_MODEL)), _const_spec((D_MODEL, IN_WIDTH)),
                  _const_spec((4, HG_WIDTH)),
                  pl.BlockSpec((tm, 3 * LANES), lambda i: (i % nt, 0))],
        out_specs=[row(w) for w in out_w],
        out_shape=[jax.ShapeDtypeStruct((n, w), dt) for w, dt in zip(out_w, out_dt)],
        compiler_params=pltpu.CompilerParams(dimension_semantics=("parallel",),
                                             vmem_limit_bytes=VMEM_LIMIT),
        name="inproj",
    )(x2, n1, w_in, lbl, rot)


def _attn_kernel(lam_ref, sub_ref, q_ref, k_ref, v_ref, o_ref, safe_ref, s_ref, p_ref):
    seq = k_ref.shape[0]
    nk = seq // ATT_K
    tiles = ATT_K // LANES
    step = pl.program_id(2)

    @pl.when(step == 0)
    def _():
        d = lax.broadcasted_iota(jnp.int32, (DA_V_DIM, LANES), 0)
        c = lax.broadcasted_iota(jnp.int32, (DA_V_DIM, LANES), 1)
        pick = jnp.where(c == d // DA_HEAD_DIM, 1.0, 0.0).astype(BF16)

        def max_norm2(ref):
            a = ref[...].astype(F32)
            return jnp.max(jnp.dot((a * a).astype(BF16), pick, preferred_element_type=F32), axis=0, keepdims=True)

        b2 = max_norm2(q_ref) * max_norm2(k_ref)
        b2 = jnp.maximum(b2[:, 0:1], b2[:, 1:2])
        safe_ref[0] = (b2[0, 0] <= 0.98 * ATT_SAFE_LOG2 * ATT_SAFE_LOG2).astype(jnp.int32)

    lv = lam_ref[...]
    lam = (jnp.exp(jnp.sum(lv[0:1] * lv[1:2], axis=-1, keepdims=True))
           - jnp.exp(jnp.sum(lv[2:3] * lv[3:4], axis=-1, keepdims=True)) + LAM_INIT)
    lane = lax.broadcasted_iota(jnp.int32, (ATT_Q, DA_V_DIM), 1)

    def masked_q(x):
        q = q_ref[pl.ds(pl.multiple_of((step * ATT_STREAMS + x) * ATT_Q, ATT_Q), ATT_Q), :]
        zero = jnp.zeros_like(q)
        return jnp.concatenate([jnp.where(lane < DA_HEAD_DIM, q, zero),
                                jnp.where(lane >= DA_HEAD_DIM, q, zero)], axis=0)

    def scores(x, qz, j, mp):
        ks = slice(j * ATT_K, (j + 1) * ATT_K)
        s = lax.dot_general(qz, k_ref[ks, :], _NT, preferred_element_type=F32)
        s_ref[x, :, ks] = s
        for t in range(tiles):
            mp = jnp.maximum(mp, s[:, t * LANES:(t + 1) * LANES])
        return mp

    def exps(x, m, j, lp):
        ks = slice(j * ATT_K, (j + 1) * ATT_K)
        p = jnp.exp2(s_ref[x, :, ks] - m)
        for t in range(tiles):
            lp = lp + p[:, t * LANES:(t + 1) * LANES]
        p_ref[x, :, ks] = p.astype(BF16)
        return lp

    def fused(x, qz, j, lp):
        ks = slice(j * ATT_K, (j + 1) * ATT_K)
        p = jnp.exp2(lax.dot_general(qz, k_ref[ks, :], _NT, preferred_element_type=F32))
        for t in range(tiles):
            lp = lp + p[:, t * LANES:(t + 1) * LANES]
        p_ref[x, :, ks] = p.astype(BF16)
        return lp

    def mix(x, rho, j, acc):
        ks = slice(j * ATT_K, (j + 1) * ATT_K)
        w = p_ref[x, :ATT_Q, ks] - rho * p_ref[x, ATT_Q:, ks]
        return acc + jnp.dot(w, v_ref[ks, :], preferred_element_type=F32)

    def finish(x, acc, lp):
        l = jnp.sum(lp, axis=-1, keepdims=True)
        if acc is None:
            return (lam * l[:ATT_Q] / l[ATT_Q:]).astype(BF16)
        o = acc * (1.0 / l[:ATT_Q])
        ms = jnp.mean(o * o, axis=-1, keepdims=True)
        o = o * lax.rsqrt(ms + SUBLN_EPS) * sub_ref[...] * (1.0 - LAM_INIT)
        o_ref[x * ATT_Q:(x + 1) * ATT_Q, :] = o.astype(o_ref.dtype)

    safe = safe_ref[0] == 1

    @pl.when(safe)
    def _():
        lps = []
        for x in range(ATT_STREAMS):
            qz = masked_q(x)
            lp = jnp.zeros((2 * ATT_Q, LANES), F32)
            for j in range(nk):
                lp = fused(x, qz, j, lp)
            lps.append(lp)
        for x in range(ATT_STREAMS):
            rho = finish(x, None, lps[x])
            acc = jnp.zeros((ATT_Q, DA_V_DIM), F32)
            for j in range(nk):
                acc = mix(x, rho, j, acc)
            finish(x, acc, lps[x])

    @pl.when(jnp.logical_not(safe))
    def _():
        qz = [None] * ATT_STREAMS
        mp = [None] * ATT_STREAMS
        lp = [None] * ATT_STREAMS
        m = [None] * ATT_STREAMS
        rho = [None] * ATT_STREAMS
        acc = [None] * ATT_STREAMS
        for t in range(ATT_STREAMS + 2):
            xs, xe, xm = t, t - 1, t - 2
            if xs < ATT_STREAMS:
                qz[xs] = masked_q(xs)
                mp[xs] = jnp.full((2 * ATT_Q, LANES), -jnp.inf, F32)
            if 0 <= xe < ATT_STREAMS:
                m[xe] = jnp.max(mp[xe], axis=-1, keepdims=True)
                lp[xe] = jnp.zeros((2 * ATT_Q, LANES), F32)
            if 0 <= xm < ATT_STREAMS:
                rho[xm] = finish(xm, None, lp[xm])
                acc[xm] = jnp.zeros((ATT_Q, DA_V_DIM), F32)
            for j in range(nk):
                if xs < ATT_STREAMS:
                    mp[xs] = scores(xs, qz[xs], j, mp[xs])
                if 0 <= xe < ATT_STREAMS:
                    lp[xe] = exps(xe, m[xe], j, lp[xe])
                if 0 <= xm < ATT_STREAMS:
                    acc[xm] = mix(xm, rho[xm], j, acc[xm])
            if 0 <= xm < ATT_STREAMS:
                finish(xm, acc[xm], lp[xm])


def _attention(lamv, subln, qd, kd, vd, batch, seq):
    n = qd.shape[0]
    tq = ATT_STREAMS * ATT_Q
    nq = seq // tq
    assert nq * tq == seq
    slab = pl.BlockSpec((seq, DA_V_DIM), lambda b, h, i: (b, h))
    return pl.pallas_call(
        _attn_kernel,
        grid=(batch, DA_HEADS, nq),
        in_specs=[pl.BlockSpec((4, DA_HEAD_DIM), lambda b, h, i: (0, 0)),
                  pl.BlockSpec((1, DA_V_DIM), lambda b, h, i: (0, 0)),
                  slab, slab, slab],
        out_specs=pl.BlockSpec((tq, DA_V_DIM), lambda b, h, i: (b * nq + i, h)),
        out_shape=jax.ShapeDtypeStruct((n, DA_WIDTH), BF16),
        scratch_shapes=[pltpu.SMEM((1,), jnp.int32),
                        pltpu.VMEM((ATT_STREAMS, 2 * ATT_Q, seq), F32),
                        pltpu.VMEM((ATT_STREAMS, 2 * ATT_Q, seq), BF16)],
        compiler_params=pltpu.CompilerParams(
            dimension_semantics=("parallel", "parallel", "arbitrary"), vmem_limit_bytes=VMEM_LIMIT),
        name="diffattn",
    )(lamv, subln, qd, kd, vd)


def _cumsum_rows(x, rev):
    c = x.shape[0]
    row = lax.broadcasted_iota(jnp.int32, x.shape, 0)
    sh = 1
    while sh < c:
        if rev:
            x = x + jnp.where(row < c - sh, pltpu.roll(x, c - sh, 0), 0.0)
        else:
            x = x + jnp.where(row >= sh, pltpu.roll(x, sh, 0), 0.0)
        sh *= 2
    return x


def _block_row(g, half, rev):
    c, w = g.shape
    idx = half if rev else half - 1
    blk = 2 * half
    if blk >= SUBLANES:
        g3 = g.reshape(c // blk, blk, w)
        return jnp.broadcast_to(g3[:, idx:idx + 1, :], g3.shape).reshape(c, w)
    g3 = g.reshape(c // SUBLANES, SUBLANES, w)
    sub = lax.broadcasted_iota(jnp.int32, g3.shape, 1)
    out = None
    for b in reversed(range(SUBLANES // blk)):
        piece = jnp.broadcast_to(g3[:, b * blk + idx:b * blk + idx + 1, :], g3.shape)
        out = piece if out is None else jnp.where(sub < (b + 1) * blk, piece, out)
    return out.reshape(c, w)


def _gla_chunk(q, f, v, st_ref, pair_level, rev):
    c = q.shape[0]
    qf = q.astype(F32)
    vf = v.astype(F32)
    kk = 1.0 - f
    g = _cumsum_rows(jnp.log2(f), rev)
    row = lax.broadcasted_iota(jnp.int32, (c, 1), 0)

    st = st_ref[...]
    o = lax.dot_general((qf * jnp.exp2(g)).astype(BF16), st.astype(BF16), _NT,
                        preferred_element_type=F32)
    o = o + jnp.sum(qf * kk, axis=-1, keepdims=True) * vf

    a = jnp.zeros((c, c), F32)
    half = 1
    lvl = 0
    while half < c:
        second = (row & half) != 0
        tgt = jnp.logical_not(second) if rev else second
        if half == 1:
            d = jnp.where(tgt, f, 1.0)
        else:
            d = jnp.exp2((g - _block_row(g, half, rev)) * jnp.where(tgt, 1.0, -1.0))
        xf = jnp.where(tgt, qf, kk) * d
        x = xf.astype(BF16)
        if half >= SUBLANES:
            first = 0 if rev else half
            pick = lambda m: jnp.concatenate(
                [m[b + first:b + first + half] for b in range(0, c, 2 * half)], axis=0)
            al = lax.dot_general(pick(xf).astype(BF16), x, _NT, preferred_element_type=F32)
            upd = jnp.where(pick(pair_level) == lvl, al, pick(a))
            rows = []
            for i, b in enumerate(range(0, c, 2 * half)):
                other = a[b + half - first:b + 2 * half - first]
                mine = upd[i * half:(i + 1) * half]
                rows += [mine, other] if rev else [other, mine]
            a = jnp.concatenate(rows, axis=0)
        else:
            al = lax.dot_general(x, x, _NT, preferred_element_type=F32)
            a = jnp.where(pair_level == lvl, al, a)
        half *= 2
        lvl += 1
    o = o + jnp.dot(a.astype(BF16), v, preferred_element_type=F32)

    g_tot = g[0:1, :] if rev else g[c - 1:c, :]
    ks = (kk * jnp.exp2(g_tot - g)).astype(BF16)
    st_ref[...] = st * jnp.exp2(g_tot) + lax.dot_general(v, ks, _TN, preferred_element_type=F32)
    return o


def _hgrn_kernel(lv_ref, qf_ref, ff_ref, vf_ref, qb_ref, fb_ref, vb_ref, of_ref, ob_ref, st_ref):
    @pl.when(pl.program_id(1) == 0)
    def _():
        st_ref[...] = jnp.zeros_like(st_ref)

    lv_fwd = lv_ref[0]
    lv_bwd = lv_ref[1]
    for r in range(HG_ROWS):
        for h in range(HG_HEADS):
            sl = slice(h * HG_HEAD_DIM, (h + 1) * HG_HEAD_DIM)
            of_ref[r, :, sl] = _gla_chunk(qf_ref[r, :, sl], ff_ref[r, :, sl], vf_ref[r, :, sl],
                                          st_ref.at[r, h], lv_fwd, False)
            ob_ref[r, :, sl] = _gla_chunk(qb_ref[r, :, sl], fb_ref[r, :, sl], vb_ref[r, :, sl],
                                          st_ref.at[r, HG_HEADS + h], lv_bwd, True)


def _pair_levels(c):
    t = jnp.arange(c, dtype=jnp.int32)[:, None]
    s = jnp.arange(c, dtype=jnp.int32)[None, :]
    x = t ^ s
    lv = jnp.full((c, c), -1, jnp.int32)
    for i in range(c.bit_length() - 1):
        lv = jnp.where((x >> i) == 1, i, lv)
    return jnp.stack([jnp.where(t > s, lv, -1), jnp.where(t < s, lv, -1)])


def _hgrn(qh, ff, fb, ih, batch, seq):
    c = HG_CHUNK
    assert batch % HG_ROWS == 0
    nc = seq // c
    fwd = pl.BlockSpec((HG_ROWS, c, HG_WIDTH), lambda b, j: (b, j, 0))
    bwd = pl.BlockSpec((HG_ROWS, c, HG_WIDTH), lambda b, j: (b, nc - 1 - j, 0))
    qh, ff, fb, ih = (a.reshape(batch, seq, HG_WIDTH) for a in (qh, ff, fb, ih))
    return pl.pallas_call(
        _hgrn_kernel,
        grid=(batch // HG_ROWS, nc),
        in_specs=[pl.BlockSpec((2, c, c), lambda b, j: (0, 0, 0)), fwd, fwd, fwd, bwd, bwd, bwd],
        out_specs=[fwd, bwd],
        out_shape=[jax.ShapeDtypeStruct((batch, seq, HG_WIDTH), F32)] * 2,
        scratch_shapes=[pltpu.VMEM((HG_ROWS, 2 * HG_HEADS, HG_HEAD_DIM, HG_HEAD_DIM), F32)],
        compiler_params=pltpu.CompilerParams(dimension_semantics=("parallel", "arbitrary"),
                                             vmem_limit_bytes=VMEM_LIMIT),
        name="hgrn2",
    )(_pair_levels(c), qh, ff, ih, qh, fb, ih)


def _merge_mlp_kernel(x_ref, of_ref, ob_ref, gh_ref, oa_ref, sgh_ref, sgd_ref,
                      hgn_ref, n2_ref, fn_ref, whg_ref, wda_ref, wout_ref, w1_ref, w2_ref, y_ref):
    o = of_ref[...] + ob_ref[...]
    parts = []
    for h in range(HG_HEADS):
        sl = slice(h * HG_HEAD_DIM, (h + 1) * HG_HEAD_DIM)
        oh = o[:, sl]
        ms = jnp.mean(oh * oh, axis=-1, keepdims=True)
        parts.append(oh * lax.rsqrt(ms + NORM_EPS) * hgn_ref[:, sl])
    o = jnp.concatenate(parts, axis=1) * gh_ref[...].astype(F32)
    y_hg = jnp.dot(o.astype(BF16), whg_ref[...], preferred_element_type=F32)
    y_da = jnp.dot(oa_ref[...], wda_ref[...], preferred_element_type=F32)
    m = sgh_ref[...].astype(F32) * y_hg + sgd_ref[...].astype(F32) * y_da
    x1 = x_ref[...] + jnp.dot(m.astype(BF16), wout_ref[...], preferred_element_type=F32)

    ms = jnp.mean(x1 * x1, axis=-1, keepdims=True)
    h2 = (x1 * lax.rsqrt(ms + NORM_EPS) * n2_ref[...]).astype(BF16)
    acc = x1
    for j in range(D_FF // MLP_COLS):
        cs = slice(j * MLP_COLS, (j + 1) * MLP_COLS)
        mid = jnp.maximum(jnp.dot(h2, w1_ref[:, cs], preferred_element_type=F32), 0.0)
        acc = acc + jnp.dot((mid * mid).astype(BF16), w2_ref[cs, :], preferred_element_type=F32)
    ms = jnp.mean(acc * acc, axis=-1, keepdims=True)
    y_ref[...] = acc * lax.rsqrt(ms + NORM_EPS) * fn_ref[...]


def _merge_mlp(x2, of, ob, gh, oa, sgh, sgd, hgn, n2, fn, whg, wda, wout, w1, w2):
    n = x2.shape[0]
    tm = MLP_ROWS
    row = lambda w: pl.BlockSpec((tm, w), lambda i: (i, 0))
    return pl.pallas_call(
        _merge_mlp_kernel,
        grid=(n // tm,),
        in_specs=[row(D_MODEL), row(HG_WIDTH), row(HG_WIDTH), row(HG_WIDTH), row(DA_WIDTH),
                  row(D_MODEL), row(D_MODEL),
                  _const_spec((1, HG_WIDTH)), _const_spec((1, D_MODEL)), _const_spec((1, D_MODEL)),
                  _const_spec((HG_WIDTH, D_MODEL)), _const_spec((DA_WIDTH, D_MODEL)),
                  _const_spec((D_MODEL, D_MODEL)), _const_spec((D_MODEL, D_FF)),
                  _const_spec((D_FF, D_MODEL))],
        out_specs=row(D_MODEL),
        out_shape=jax.ShapeDtypeStruct((n, D_MODEL), F32),
        compiler_params=pltpu.CompilerParams(dimension_semantics=("parallel",),
                                             vmem_limit_bytes=VMEM_LIMIT),
        name="merge_mlp",
    )(x2, of, ob, gh, oa, sgh, sgd, hgn, n2, fn, whg, wda, wout, w1, w2)


def _rotary_table(seq):
    pos = jnp.arange(seq, dtype=F32)
    inv_freq = ROPE_THETA ** (-jnp.arange(0, ROT_DIM, 2, dtype=F32) / ROT_DIM)
    ang = pos[:, None] * inv_freq[None, :]
    cos, sin = jnp.cos(ang), jnp.sin(ang)
    pad = jnp.zeros((seq, DA_HEAD_DIM - ROT_DIM), F32)
    zeros = jnp.zeros_like(sin)
    cos64 = jnp.concatenate([cos, cos, pad + 1.0], axis=1)
    lo64 = jnp.concatenate([zeros, sin, pad], axis=1)
    hi64 = jnp.concatenate([-sin, zeros, pad], axis=1)
    return jnp.concatenate([cos64, cos64, lo64, lo64, hi64, hi64], axis=1)


def _trunk(x, p):
    batch, seq, _ = x.shape
    assert seq % max(IN_ROWS, ATT_STREAMS * ATT_Q, ATT_K, HG_CHUNK, MLP_ROWS) == 0
    x2 = x.reshape(batch * seq, D_MODEL)
    qd, kd, vd, qh, ff, fb, ih, gh, sgh, sgd = _inproj(x2, p["n1"], p["w_in"], p["lbl"],
                                                       _rotary_table(seq), seq)
    oa = _attention(p["lamv"], p["subln"], qd, kd, vd, batch, seq)
    of, ob = (a.reshape(batch * seq, HG_WIDTH) for a in _hgrn(qh, ff, fb, ih, batch, seq))
    y = _merge_mlp(x2, of, ob, gh, oa, sgh, sgd, p["hgn"], p["n2"], p["fn"],
                   p["whg"], p["wda"], p["wout"], p["w1"], p["w2"])
    return y.reshape(batch, seq, D_MODEL)


def kernel(x_prompt, x_sample, norm1, w_in, hg_lb_logits, hg_norm, w_hg_branch, da_lambda_q1, da_lambda_k1, da_lambda_q2, da_lambda_k2, da_subln, w_da_branch, w_out, norm2, w_mlp_in, w_mlp_out, final_norm):
    assert norm1.shape[0] == 1 and hg_lb_logits.shape == (2, 2, HG_WIDTH)
    p = dict(
        n1=norm1.reshape(1, D_MODEL), w_in=w_in[0].astype(BF16),
        lbl=hg_lb_logits.reshape(4, HG_WIDTH),
        lamv=jnp.concatenate([da_lambda_q1, da_lambda_k1, da_lambda_q2, da_lambda_k2], axis=0),
        subln=da_subln.reshape(1, DA_V_DIM), hgn=hg_norm.reshape(1, HG_WIDTH),
        n2=norm2.reshape(1, D_MODEL), fn=final_norm.reshape(1, D_MODEL),
        whg=w_hg_branch[0].astype(BF16), wda=w_da_branch[0].astype(BF16), wout=w_out[0].astype(BF16),
        w1=w_mlp_in[0].astype(BF16), w2=w_mlp_out[0].astype(BF16),
    )
    return _trunk(x_prompt, p), _trunk(x_sample, p)
```

```python
import math

import jax
import jax.numpy as jnp
from jax import lax
from jax.experimental import pallas as pl
from jax.experimental.pallas import tpu as pltpu

F32 = jnp.float32
BF16 = jnp.bfloat16

D_MODEL = 1024
HG_HEADS = 4
HG_HEAD_DIM = 128
HG_WIDTH = HG_HEADS * HG_HEAD_DIM
HG_SCALE = HG_HEAD_DIM ** -0.5
DA_HEADS = 4
DA_HEAD_DIM = 64
DA_V_DIM = 2 * DA_HEAD_DIM
DA_WIDTH = DA_HEADS * DA_V_DIM
DA_SCALE = DA_HEAD_DIM ** -0.5
ROT_DIM = DA_HEAD_DIM // 4
ROT_HALF = ROT_DIM // 2
ROPE_THETA = 500000.0
D_FF = 4 * D_MODEL
NORM_EPS = 1e-6
SUBLN_EPS = 1e-5
IN_WIDTH = 3 * DA_WIDTH + 5 * HG_WIDTH + 2 * D_MODEL
LAM_INIT = 0.8 - 0.6 * math.exp(-0.3 * 0)

LANES = 128
SUBLANES = 8
VMEM_LIMIT = 56 * 1024 * 1024

SEG = 512
IN_ROWS = 1024
ATT_Q = 128
ATT_STREAMS = 4
ATT_K = 512
LOG2_E = math.log2(math.e)
ATT_SAFE_LOG2 = 48.0
HG_CHUNK = 128
HG_ROWS = 4
MLP_ROWS = 512
MLP_COLS = 1024

_NT = (((1,), (1,)), ((), ()))
_TN = (((0,), (0,)), ((), ()))


def _sigmoid(x):
    return 1.0 / (1.0 + jnp.exp(-x))


def _const_spec(shape):
    nd = len(shape)
    return pl.BlockSpec(shape, lambda *_: (0,) * nd, pipeline_mode=pl.Buffered(1))


def _inproj_kernel(x_ref, n1_ref, w_ref, lbl_ref, rot_ref,
                   qd_ref, kd_ref, vd_ref, qh_ref, ff_ref, fb_ref, ih_ref, gh_ref, sgh_ref, sgd_ref):
    x = x_ref[...]
    ms = jnp.mean(x * x, axis=-1, keepdims=True)
    h = (x * lax.rsqrt(ms + NORM_EPS) * n1_ref[...]).astype(BF16)

    def seg(j):
        return jnp.dot(h, w_ref[:, j * SEG:(j + 1) * SEG], preferred_element_type=F32)

    cos_t = rot_ref[:, 0:LANES]
    sin_lo = rot_ref[:, LANES:2 * LANES]
    sin_hi = rot_ref[:, 2 * LANES:3 * LANES]

    def rope_store(u, out_ref, scale):
        for p in range(SEG // LANES):
            a = u[:, p * LANES:(p + 1) * LANES]
            r = (a * cos_t + pltpu.roll(a, ROT_HALF, 1) * sin_lo
                 + pltpu.roll(a, LANES - ROT_HALF, 1) * sin_hi)
            if scale != 1.0:
                r = r * scale
            out_ref[:, p * LANES:(p + 1) * LANES] = r.astype(out_ref.dtype)

    rope_store(seg(0), qd_ref, DA_SCALE * LOG2_E)
    rope_store(seg(1), kd_ref, 1.0)
    vd_ref[...] = seg(2).astype(BF16)

    u = seg(3)
    qh_ref[...] = (u * _sigmoid(u) * HG_SCALE).astype(BF16)

    def lower_bound(d):
        l0 = lbl_ref[2 * d:2 * d + 1, :]
        l1 = lbl_ref[2 * d + 1:2 * d + 2, :]
        m = jnp.maximum(l0, l1)
        e0 = jnp.exp(l0 - m)
        e1 = jnp.exp(l1 - m)
        return e0 / (e0 + e1)

    lb = lower_bound(0)
    ff_ref[...] = lb + (1.0 - lb) * _sigmoid(seg(4))
    lb = lower_bound(1)
    fb_ref[...] = lb + (1.0 - lb) * _sigmoid(seg(5))
    ih_ref[...] = seg(6).astype(BF16)
    u = seg(7)
    gh_ref[...] = (u * _sigmoid(u)).astype(BF16)
    for p in range(D_MODEL // SEG):
        sgh_ref[:, p * SEG:(p + 1) * SEG] = _sigmoid(seg(8 + p)).astype(BF16)
        sgd_ref[:, p * SEG:(p + 1) * SEG] = _sigmoid(seg(10 + p)).astype(BF16)


def _inproj(x2, n1, w_in, lbl, rot, seq):
    n = x2.shape[0]
    tm = min(IN_ROWS, seq)
    nt = seq // tm
    row = lambda w: pl.BlockSpec((tm, w), lambda i: (i, 0))
    out_w = (DA_WIDTH,) * 3 + (HG_WIDTH,) * 5 + (D_MODEL,) * 2
    out_dt = (BF16, BF16, BF16, BF16, F32, F32, BF16, BF16, BF16, BF16)
    return pl.pallas_call(
        _inproj_kernel,
        grid=(n // tm,),
        in_specs=[row(D_MODEL), _const_spec((1, D_MODEL)), _const_spec((D_MODEL, IN_WIDTH)),
                  _const_spec((4, HG_WIDTH)),
                  pl.BlockSpec((tm, 3 * LANES), lambda i: (i % nt, 0))],
        out_specs=[row(w) for w in out_w],
        out_shape=[jax.ShapeDtypeStruct((n, w), dt) for w, dt in zip(out_w, out_dt)],
        compiler_params=pltpu.CompilerParams(dimension_semantics=("parallel",),
                                             vmem_limit_bytes=VMEM_LIMIT),
        name="inproj",
    )(x2, n1, w_in, lbl, rot)


def _attn_kernel(lam_ref, sub_ref, q_ref, k_ref, v_ref, o_ref, safe_ref, s_ref, p_ref):
    seq = k_ref.shape[0]
    nk = seq // ATT_K
    tiles = ATT_K // LANES
    step = pl.program_id(2)

    @pl.when(step == 0)
    def _():
        d = lax.broadcasted_iota(jnp.int32, (DA_V_DIM, LANES), 0)
        c = lax.broadcasted_iota(jnp.int32, (DA_V_DIM, LANES), 1)
        pick = jnp.where(c == d // DA_HEAD_DIM, 1.0, 0.0).astype(BF16)

        def max_norm2(ref):
            a = ref[...].astype(F32)
            return jnp.max(jnp.dot((a * a).astype(BF16), pick, preferred_element_type=F32), axis=0, keepdims=True)

        b2 = max_norm2(q_ref) * max_norm2(k_ref)
        b2 = jnp.maximum(b2[:, 0:1], b2[:, 1:2])
        safe_ref[0] = (b2[0, 0] <= 0.98 * ATT_SAFE_LOG2 * ATT_SAFE_LOG2).astype(jnp.int32)

    lv = lam_ref[...]
    lam = (jnp.exp(jnp.sum(lv[0:1] * lv[1:2], axis=-1, keepdims=True))
           - jnp.exp(jnp.sum(lv[2:3] * lv[3:4], axis=-1, keepdims=True)) + LAM_INIT)
    lane = lax.broadcasted_iota(jnp.int32, (ATT_Q, DA_V_DIM), 1)

    def masked_q(x):
        q = q_ref[pl.ds(pl.multiple_of((step * ATT_STREAMS + x) * ATT_Q, ATT_Q), ATT_Q), :]
        zero = jnp.zeros_like(q)
        return jnp.concatenate([jnp.where(lane < DA_HEAD_DIM, q, zero),
                                jnp.where(lane >= DA_HEAD_DIM, q, zero)], axis=0)

    def scores(x, qz, j, mp):
        ks = slice(j * ATT_K, (j + 1) * ATT_K)
        s = lax.dot_general(qz, k_ref[ks, :], _NT, preferred_element_type=F32)
        s_ref[x % 2, :, ks] = s
        for t in range(tiles):
            mp = jnp.maximum(mp, s[:, t * LANES:(t + 1) * LANES])
        return mp

    def exps(x, m, j, lp):
        ks = slice(j * ATT_K, (j + 1) * ATT_K)
        p = jnp.exp2(s_ref[x % 2, :, ks] - m)
        for t in range(tiles):
            lp = lp + p[:, t * LANES:(t + 1) * LANES]
        p_ref[x % 2, :, ks] = p.astype(BF16)
        return lp

    def fused(x, qz, j, lp):
        ks = slice(j * ATT_K, (j + 1) * ATT_K)
        p = jnp.exp2(lax.dot_general(qz, k_ref[ks, :], _NT, preferred_element_type=F32))
        for t in range(tiles):
            lp = lp + p[:, t * LANES:(t + 1) * LANES]
        p_ref[x % 2, :, ks] = p.astype(BF16)
        return lp

    def mix(x, rho, j, acc):
        ks = slice(j * ATT_K, (j + 1) * ATT_K)
        w = p_ref[x % 2, :ATT_Q, ks] - rho * p_ref[x % 2, ATT_Q:, ks]
        return acc + jnp.dot(w, v_ref[ks, :], preferred_element_type=F32)

    def finish(x, acc, lp):
        l = jnp.sum(lp, axis=-1, keepdims=True)
        if acc is None:
            return (lam * l[:ATT_Q] / l[ATT_Q:]).astype(BF16)
        o = acc * (1.0 / l[:ATT_Q])
        ms = jnp.mean(o * o, axis=-1, keepdims=True)
        o = o * lax.rsqrt(ms + SUBLN_EPS) * sub_ref[...] * (1.0 - LAM_INIT)
        o_ref[x * ATT_Q:(x + 1) * ATT_Q, :] = o.astype(o_ref.dtype)

    safe = safe_ref[0] == 1

    @pl.when(safe)
    def _():
        lps = [None] * ATT_STREAMS
        for t in range(ATT_STREAMS + 1):
            if t < ATT_STREAMS:
                qz = masked_q(t)
                lp = jnp.zeros((2 * ATT_Q, LANES), F32)
                for j in range(nk):
                    lp = fused(t, qz, j, lp)
                lps[t] = lp
            if t >= 1:
                x = t - 1
                rho = finish(x, None, lps[x])
                acc = jnp.zeros((ATT_Q, DA_V_DIM), F32)
                for j in range(nk):
                    acc = mix(x, rho, j, acc)
                finish(x, acc, lps[x])

    @pl.when(jnp.logical_not(safe))
    def _():
        qz = [None] * ATT_STREAMS
        mp = [None] * ATT_STREAMS
        lp = [None] * ATT_STREAMS
        m = [None] * ATT_STREAMS
        rho = [None] * ATT_STREAMS
        acc = [None] * ATT_STREAMS
        for t in range(ATT_STREAMS + 2):
            xs, xe, xm = t, t - 1, t - 2
            if xs < ATT_STREAMS:
                qz[xs] = masked_q(xs)
                mp[xs] = jnp.full((2 * ATT_Q, LANES), -jnp.inf, F32)
            if 0 <= xe < ATT_STREAMS:
                m[xe] = jnp.max(mp[xe], axis=-1, keepdims=True)
                lp[xe] = jnp.zeros((2 * ATT_Q, LANES), F32)
            if 0 <= xm < ATT_STREAMS:
                rho[xm] = finish(xm, None, lp[xm])
                acc[xm] = jnp.zeros((ATT_Q, DA_V_DIM), F32)
            for j in range(nk):
                if xs < ATT_STREAMS:
                    mp[xs] = scores(xs, qz[xs], j, mp[xs])
                if 0 <= xe < ATT_STREAMS:
                    lp[xe] = exps(xe, m[xe], j, lp[xe])
                if 0 <= xm < ATT_STREAMS:
                    acc[xm] = mix(xm, rho[xm], j, acc[xm])
            if 0 <= xm < ATT_STREAMS:
                finish(xm, acc[xm], lp[xm])


def _attention(lamv, subln, qd, kd, vd, batch, seq):
    n = qd.shape[0]
    tq = ATT_STREAMS * ATT_Q
    nq = seq // tq
    assert nq * tq == seq
    slab = pl.BlockSpec((seq, DA_V_DIM), lambda b, h, i: (b, h))
    return pl.pallas_call(
        _attn_kernel,
        grid=(batch, DA_HEADS, nq),
        in_specs=[pl.BlockSpec((4, DA_HEAD_DIM), lambda b, h, i: (0, 0)),
                  pl.BlockSpec((1, DA_V_DIM), lambda b, h, i: (0, 0)),
                  slab, slab, slab],
        out_specs=pl.BlockSpec((tq, DA_V_DIM), lambda b, h, i: (b * nq + i, h)),
        out_shape=jax.ShapeDtypeStruct((n, DA_WIDTH), BF16),
        scratch_shapes=[pltpu.SMEM((1,), jnp.int32),
                        pltpu.VMEM((2, 2 * ATT_Q, seq), F32),
                        pltpu.VMEM((2, 2 * ATT_Q, seq), BF16)],
        compiler_params=pltpu.CompilerParams(
            dimension_semantics=("parallel", "parallel", "arbitrary"), vmem_limit_bytes=VMEM_LIMIT),
        name="diffattn",
    )(lamv, subln, qd, kd, vd)


def _cumsum_rows(x, rev):
    c = x.shape[0]
    row = lax.broadcasted_iota(jnp.int32, x.shape, 0)
    sh = 1
    while sh < c:
        if rev:
            x = x + jnp.where(row < c - sh, pltpu.roll(x, c - sh, 0), 0.0)
        else:
            x = x + jnp.where(row >= sh, pltpu.roll(x, sh, 0), 0.0)
        sh *= 2
    return x


def _block_row(g, half, rev):
    c, w = g.shape
    idx = half if rev else half - 1
    blk = 2 * half
    if blk >= SUBLANES:
        g3 = g.reshape(c // blk, blk, w)
        return jnp.broadcast_to(g3[:, idx:idx + 1, :], g3.shape).reshape(c, w)
    g3 = g.reshape(c // SUBLANES, SUBLANES, w)
    sub = lax.broadcasted_iota(jnp.int32, g3.shape, 1)
    out = None
    for b in reversed(range(SUBLANES // blk)):
        piece = jnp.broadcast_to(g3[:, b * blk + idx:b * blk + idx + 1, :], g3.shape)
        out = piece if out is None else jnp.where(sub < (b + 1) * blk, piece, out)
    return out.reshape(c, w)


def _gla_chunk(q, f, v, st_ref, pair_level, rev):
    c = q.shape[0]
    qf = q.astype(F32)
    vf = v.astype(F32)
    kk = 1.0 - f
    g = _cumsum_rows(jnp.log2(f), rev)
    row = lax.broadcasted_iota(jnp.int32, (c, 1), 0)

    st = st_ref[...]
    o = lax.dot_general((qf * jnp.exp2(g)).astype(BF16), st.astype(BF16), _NT,
                        preferred_element_type=F32)
    o = o + jnp.sum(qf * kk, axis=-1, keepdims=True) * vf

    a = jnp.zeros((c, c), F32)
    half = 1
    lvl = 0
    while half < c:
        second = (row & half) != 0
        tgt = jnp.logical_not(second) if rev else second
        if half == 1:
            d = jnp.where(tgt, f, 1.0)
        else:
            d = jnp.exp2((g - _block_row(g, half, rev)) * jnp.where(tgt, 1.0, -1.0))
        xf = jnp.where(tgt, qf, kk) * d
        x = xf.astype(BF16)
        if half >= SUBLANES:
            first = 0 if rev else half
            pick = lambda m: jnp.concatenate(
                [m[b + first:b + first + half] for b in range(0, c, 2 * half)], axis=0)
            al = lax.dot_general(pick(xf).astype(BF16), x, _NT, preferred_element_type=F32)
            upd = jnp.where(pick(pair_level) == lvl, al, pick(a))
            rows = []
            for i, b in enumerate(range(0, c, 2 * half)):
                other = a[b + half - first:b + 2 * half - first]
                mine = upd[i * half:(i + 1) * half]
                rows += [mine, other] if rev else [other, mine]
            a = jnp.concatenate(rows, axis=0)
        else:
            al = lax.dot_general(x, x, _NT, preferred_element_type=F32)
            a = jnp.where(pair_level == lvl, al, a)
        half *= 2
        lvl += 1
    o = o + jnp.dot(a.astype(BF16), v, preferred_element_type=F32)

    g_tot = g[0:1, :] if rev else g[c - 1:c, :]
    ks = (kk * jnp.exp2(g_tot - g)).astype(BF16)
    st_ref[...] = st * jnp.exp2(g_tot) + lax.dot_general(v, ks, _TN, preferred_element_type=F32)
    return o


def _hgrn_kernel(lv_ref, qf_ref, ff_ref, vf_ref, qb_ref, fb_ref, vb_ref, of_ref, ob_ref, st_ref):
    @pl.when(pl.program_id(1) == 0)
    def _():
        st_ref[...] = jnp.zeros_like(st_ref)

    lv_fwd = lv_ref[0]
    lv_bwd = lv_ref[1]
    for r in range(HG_ROWS):
        for h in range(HG_HEADS):
            sl = slice(h * HG_HEAD_DIM, (h + 1) * HG_HEAD_DIM)
            of_ref[r, :, sl] = _gla_chunk(qf_ref[r, :, sl], ff_ref[r, :, sl], vf_ref[r, :, sl],
                                          st_ref.at[r, h], lv_fwd, False)
            ob_ref[r, :, sl] = _gla_chunk(qb_ref[r, :, sl], fb_ref[r, :, sl], vb_ref[r, :, sl],
                                          st_ref.at[r, HG_HEADS + h], lv_bwd, True)


def _pair_levels(c):
    t = jnp.arange(c, dtype=jnp.int32)[:, None]
    s = jnp.arange(c, dtype=jnp.int32)[None, :]
    x = t ^ s
    lv = jnp.full((c, c), -1, jnp.int32)
    for i in range(c.bit_length() - 1):
        lv = jnp.where((x >> i) == 1, i, lv)
    return jnp.stack([jnp.where(t > s, lv, -1), jnp.where(t < s, lv, -1)])


def _hgrn(qh, ff, fb, ih, batch, seq):
    c = HG_CHUNK
    assert batch % HG_ROWS == 0
    nc = seq // c
    fwd = pl.BlockSpec((HG_ROWS, c, HG_WIDTH), lambda b, j: (b, j, 0))
    bwd = pl.BlockSpec((HG_ROWS, c, HG_WIDTH), lambda b, j: (b, nc - 1 - j, 0))
    qh, ff, fb, ih = (a.reshape(batch, seq, HG_WIDTH) for a in (qh, ff, fb, ih))
    return pl.pallas_call(
        _hgrn_kernel,
        grid=(batch // HG_ROWS, nc),
        in_specs=[pl.BlockSpec((2, c, c), lambda b, j: (0, 0, 0)), fwd, fwd, fwd, bwd, bwd, bwd],
        out_specs=[fwd, bwd],
        out_shape=[jax.ShapeDtypeStruct((batch, seq, HG_WIDTH), F32)] * 2,
        scratch_shapes=[pltpu.VMEM((HG_ROWS, 2 * HG_HEADS, HG_HEAD_DIM, HG_HEAD_DIM), F32)],
        compiler_params=pltpu.CompilerParams(dimension_semantics=("parallel", "arbitrary"),
                                             vmem_limit_bytes=VMEM_LIMIT),
        name="hgrn2",
    )(_pair_levels(c), qh, ff, ih, qh, fb, ih)


def _merge_mlp_kernel(x_ref, of_ref, ob_ref, gh_ref, oa_ref, sgh_ref, sgd_ref,
                      hgn_ref, n2_ref, fn_ref, whg_ref, wda_ref, wout_ref, w1_ref, w2_ref, y_ref):
    o = of_ref[...] + ob_ref[...]
    parts = []
    for h in range(HG_HEADS):
        sl = slice(h * HG_HEAD_DIM, (h + 1) * HG_HEAD_DIM)
        oh = o[:, sl]
        ms = jnp.mean(oh * oh, axis=-1, keepdims=True)
        parts.append(oh * lax.rsqrt(ms + NORM_EPS) * hgn_ref[:, sl])
    o = jnp.concatenate(parts, axis=1) * gh_ref[...].astype(F32)
    y_hg = jnp.dot(o.astype(BF16), whg_ref[...], preferred_element_type=F32)
    y_da = jnp.dot(oa_ref[...], wda_ref[...], preferred_element_type=F32)
    m = sgh_ref[...].astype(F32) * y_hg + sgd_ref[...].astype(F32) * y_da
    x1 = x_ref[...] + jnp.dot(m.astype(BF16), wout_ref[...], preferred_element_type=F32)

    ms = jnp.mean(x1 * x1, axis=-1, keepdims=True)
    h2 = (x1 * lax.rsqrt(ms + NORM_EPS) * n2_ref[...]).astype(BF16)
    acc = x1
    for j in range(D_FF // MLP_COLS):
        cs = slice(j * MLP_COLS, (j + 1) * MLP_COLS)
        mid = jnp.maximum(jnp.dot(h2, w1_ref[:, cs], preferred_element_type=F32), 0.0)
        acc = acc + jnp.dot((mid * mid).astype(BF16), w2_ref[cs, :], preferred_element_type=F32)
    ms = jnp.mean(acc * acc, axis=-1, keepdims=True)
    y_ref[...] = acc * lax.rsqrt(ms + NORM_EPS) * fn_ref[...]


def _merge_mlp(x2, of, ob, gh, oa, sgh, sgd, hgn, n2, fn, whg, wda, wout, w1, w2):
    n = x2.shape[0]
    tm = MLP_ROWS
    row = lambda w: pl.BlockSpec((tm, w), lambda i: (i, 0))
    return pl.pallas_call(
        _merge_mlp_kernel,
        grid=(n // tm,),
        in_specs=[row(D_MODEL), row(HG_WIDTH), row(HG_WIDTH), row(HG_WIDTH), row(DA_WIDTH),
                  row(D_MODEL), row(D_MODEL),
                  _const_spec((1, HG_WIDTH)), _const_spec((1, D_MODEL)), _const_spec((1, D_MODEL)),
                  _const_spec((HG_WIDTH, D_MODEL)), _const_spec((DA_WIDTH, D_MODEL)),
                  _const_spec((D_MODEL, D_MODEL)), _const_spec((D_MODEL, D_FF)),
                  _const_spec((D_FF, D_MODEL))],
        out_specs=row(D_MODEL),
        out_shape=jax.ShapeDtypeStruct((n, D_MODEL), F32),
        compiler_params=pltpu.CompilerParams(dimension_semantics=("parallel",),
                                             vmem_limit_bytes=VMEM_LIMIT),
        name="merge_mlp",
    )(x2, of, ob, gh, oa, sgh, sgd, hgn, n2, fn, whg, wda, wout, w1, w2)


def _rotary_table(seq):
    pos = jnp.arange(seq, dtype=F32)
    inv_freq = ROPE_THETA ** (-jnp.arange(0, ROT_DIM, 2, dtype=F32) / ROT_DIM)
    ang = pos[:, None] * inv_freq[None, :]
    cos, sin = jnp.cos(ang), jnp.sin(ang)
    pad = jnp.zeros((seq, DA_HEAD_DIM - ROT_DIM), F32)
    zeros = jnp.zeros_like(sin)
    cos64 = jnp.concatenate([cos, cos, pad + 1.0], axis=1)
    lo64 = jnp.concatenate([zeros, sin, pad], axis=1)
    hi64 = jnp.concatenate([-sin, zeros, pad], axis=1)
    return jnp.concatenate([cos64, cos64, lo64, lo64, hi64, hi64], axis=1)


def _trunk(x, p):
    batch, seq, _ = x.shape
    assert seq % max(IN_ROWS, ATT_STREAMS * ATT_Q, ATT_K, HG_CHUNK, MLP_ROWS) == 0
    x2 = x.reshape(batch * seq, D_MODEL)
    qd, kd, vd, qh, ff, fb, ih, gh, sgh, sgd = _inproj(x2, p["n1"], p["w_in"], p["lbl"],
                                                       _rotary_table(seq), seq)
    oa = _attention(p["lamv"], p["subln"], qd, kd, vd, batch, seq)
    of, ob = (a.reshape(batch * seq, HG_WIDTH) for a in _hgrn(qh, ff, fb, ih, batch, seq))
    y = _merge_mlp(x2, of, ob, gh, oa, sgh, sgd, p["hgn"], p["n2"], p["fn"],
                   p["whg"], p["wda"], p["wout"], p["w1"], p["w2"])
    return y.reshape(batch, seq, D_MODEL)


def kernel(x_prompt, x_sample, norm1, w_in, hg_lb_logits, hg_norm, w_hg_branch, da_lambda_q1, da_lambda_k1, da_lambda_q2, da_lambda_k2, da_subln, w_da_branch, w_out, norm2, w_mlp_in, w_mlp_out, final_norm):
    assert norm1.shape[0] == 1 and hg_lb_logits.shape == (2, 2, HG_WIDTH)
    p = dict(
        n1=norm1.reshape(1, D_MODEL), w_in=w_in[0].astype(BF16),
        lbl=hg_lb_logits.reshape(4, HG_WIDTH),
        lamv=jnp.concatenate([da_lambda_q1, da_lambda_k1, da_lambda_q2, da_lambda_k2], axis=0),
        subln=da_subln.reshape(1, DA_V_DIM), hgn=hg_norm.reshape(1, HG_WIDTH),
        n2=norm2.reshape(1, D_MODEL), fn=final_norm.reshape(1, D_MODEL),
        whg=w_hg_branch[0].astype(BF16), wda=w_da_branch[0].astype(BF16), wout=w_out[0].astype(BF16),
        w1=w_mlp_in[0].astype(BF16), w2=w_mlp_out[0].astype(BF16),
    )
    return _trunk(x_prompt, p), _trunk(x_sample, p)
```

```python
import math

import jax
import jax.numpy as jnp
from jax import lax
from jax.experimental import pallas as pl
from jax.experimental.pallas import tpu as pltpu

F32 = jnp.float32
BF16 = jnp.bfloat16

D_MODEL = 1024
HG_HEADS = 4
HG_HEAD_DIM = 128
HG_WIDTH = HG_HEADS * HG_HEAD_DIM
HG_SCALE = HG_HEAD_DIM ** -0.5
DA_HEADS = 4
DA_HEAD_DIM = 64
DA_V_DIM = 2 * DA_HEAD_DIM
DA_WIDTH = DA_HEADS * DA_V_DIM
DA_SCALE = DA_HEAD_DIM ** -0.5
ROT_DIM = DA_HEAD_DIM // 4
ROT_HALF = ROT_DIM // 2
ROPE_THETA = 500000.0
D_FF = 4 * D_MODEL
NORM_EPS = 1e-6
SUBLN_EPS = 1e-5
IN_WIDTH = 3 * DA_WIDTH + 5 * HG_WIDTH + 2 * D_MODEL
LAM_INIT = 0.8 - 0.6 * math.exp(-0.3 * 0)

LANES = 128
SUBLANES = 8
VMEM_LIMIT = 56 * 1024 * 1024

SEG = 512
IN_ROWS = 1024
ATT_Q = 128
ATT_STREAMS = 4
ATT_K = 512
LOG2_E = math.log2(math.e)
ATT_SAFE_LOG2 = 48.0
HG_CHUNK = 128
HG_ROWS = 4
MLP_ROWS = 512
MLP_COLS = 1024

_NT = (((1,), (1,)), ((), ()))
_TN = (((0,), (0,)), ((), ()))


def _sigmoid(x):
    return 1.0 / (1.0 + jnp.exp(-x))


def _const_spec(shape):
    nd = len(shape)
    return pl.BlockSpec(shape, lambda *_: (0,) * nd, pipeline_mode=pl.Buffered(1))


def _inproj_kernel(x_ref, n1_ref, w_ref, lbl_ref, rot_ref,
                   qd_ref, kd_ref, vd_ref, qh_ref, ff_ref, fb_ref, ih_ref, gh_ref, sgh_ref, sgd_ref):
    x = x_ref[...]
    ms = jnp.mean(x * x, axis=-1, keepdims=True)
    h = (x * lax.rsqrt(ms + NORM_EPS) * n1_ref[...]).astype(BF16)

    def seg(j):
        return jnp.dot(h, w_ref[:, j * SEG:(j + 1) * SEG], preferred_element_type=F32)

    cos_t = rot_ref[:, 0:LANES]
    sin_lo = rot_ref[:, LANES:2 * LANES]
    sin_hi = rot_ref[:, 2 * LANES:3 * LANES]

    def rope_store(u, out_ref, scale):
        for p in range(SEG // LANES):
            a = u[:, p * LANES:(p + 1) * LANES]
            r = (a * cos_t + pltpu.roll(a, ROT_HALF, 1) * sin_lo
                 + pltpu.roll(a, LANES - ROT_HALF, 1) * sin_hi)
            if scale != 1.0:
                r = r * scale
            out_ref[:, p * LANES:(p + 1) * LANES] = r.astype(out_ref.dtype)

    rope_store(seg(0), qd_ref, DA_SCALE * LOG2_E)
    rope_store(seg(1), kd_ref, 1.0)
    vd_ref[...] = seg(2).astype(BF16)

    u = seg(3)
    qh_ref[...] = (u * _sigmoid(u) * HG_SCALE).astype(BF16)

    def lower_bound(d):
        l0 = lbl_ref[2 * d:2 * d + 1, :]
        l1 = lbl_ref[2 * d + 1:2 * d + 2, :]
        m = jnp.maximum(l0, l1)
        e0 = jnp.exp(l0 - m)
        e1 = jnp.exp(l1 - m)
        return e0 / (e0 + e1)

    lb = lower_bound(0)
    ff_ref[...] = lb + (1.0 - lb) * _sigmoid(seg(4))
    lb = lower_bound(1)
    fb_ref[...] = lb + (1.0 - lb) * _sigmoid(seg(5))
    ih_ref[...] = seg(6).astype(BF16)
    u = seg(7)
    gh_ref[...] = (u * _sigmoid(u)).astype(BF16)
    for p in range(D_MODEL // SEG):
        sgh_ref[:, p * SEG:(p + 1) * SEG] = _sigmoid(seg(8 + p)).astype(BF16)
        sgd_ref[:, p * SEG:(p + 1) * SEG] = _sigmoid(seg(10 + p)).astype(BF16)


def _inproj(x2, n1, w_in, lbl, rot, seq):
    n = x2.shape[0]
    tm = min(IN_ROWS, seq)
    nt = seq // tm
    row = lambda w: pl.BlockSpec((tm, w), lambda i: (i, 0))
    out_w = (DA_WIDTH,) * 3 + (HG_WIDTH,) * 5 + (D_MODEL,) * 2
    out_dt = (BF16, BF16, BF16, BF16, F32, F32, BF16, BF16, BF16, BF16)
    return pl.pallas_call(
        _inproj_kernel,
        grid=(n // tm,),
        in_specs=[row(D_MODEL), _const_spec((1, D_MODEL)), _const_spec((D_MODEL, IN_WIDTH)),
                  _const_spec((4, HG_WIDTH)),
                  pl.BlockSpec((tm, 3 * LANES), lambda i: (i % nt, 0))],
        out_specs=[row(w) for w in out_w],
        out_shape=[jax.ShapeDtypeStruct((n, w), dt) for w, dt in zip(out_w, out_dt)],
        compiler_params=pltpu.CompilerParams(dimension_semantics=("parallel",),
                                             vmem_limit_bytes=VMEM_LIMIT),
        name="inproj",
    )(x2, n1, w_in, lbl, rot)


def _attn_kernel(lam_ref, sub_ref, q_ref, k_ref, v_ref, o_ref, s_ref, p_ref):
    seq = k_ref.shape[0]
    nk = seq // ATT_K
    tiles = ATT_K // LANES

    def bound_check():
        d = lax.broadcasted_iota(jnp.int32, (DA_V_DIM, LANES), 0)
        c = lax.broadcasted_iota(jnp.int32, (DA_V_DIM, LANES), 1)
        pick = jnp.where(c == d // DA_HEAD_DIM, 1.0, 0.0).astype(BF16)

        def max_norm2(ref):
            a = ref[...].astype(F32)
            return jnp.max(jnp.dot((a * a).astype(BF16), pick, preferred_element_type=F32), axis=0, keepdims=True)

        b2 = max_norm2(q_ref) * max_norm2(k_ref)
        b2 = jnp.maximum(b2[:, 0:1], b2[:, 1:2])
        return b2[0, 0] <= 0.98 * ATT_SAFE_LOG2 * ATT_SAFE_LOG2

    safe = bound_check()

    lv = lam_ref[...]
    lam = (jnp.exp(jnp.sum(lv[0:1] * lv[1:2], axis=-1, keepdims=True))
           - jnp.exp(jnp.sum(lv[2:3] * lv[3:4], axis=-1, keepdims=True)) + LAM_INIT)
    lane = lax.broadcasted_iota(jnp.int32, (ATT_Q, DA_V_DIM), 1)

    def block_rows(step, x):
        return pl.ds(pl.multiple_of((step * ATT_STREAMS + x) * ATT_Q, ATT_Q), ATT_Q)

    def masked_q(step, x):
        q = q_ref[block_rows(step, x), :]
        zero = jnp.zeros_like(q)
        return jnp.concatenate([jnp.where(lane < DA_HEAD_DIM, q, zero),
                                jnp.where(lane >= DA_HEAD_DIM, q, zero)], axis=0)

    def scores(x, qz, j, mp):
        ks = slice(j * ATT_K, (j + 1) * ATT_K)
        s = lax.dot_general(qz, k_ref[ks, :], _NT, preferred_element_type=F32)
        s_ref[x % 2, :, ks] = s
        for t in range(tiles):
            mp = jnp.maximum(mp, s[:, t * LANES:(t + 1) * LANES])
        return mp

    def exps(x, m, j, lp):
        ks = slice(j * ATT_K, (j + 1) * ATT_K)
        p = jnp.exp2(s_ref[x % 2, :, ks] - m)
        for t in range(tiles):
            lp = lp + p[:, t * LANES:(t + 1) * LANES]
        p_ref[x % 2, :, ks] = p.astype(BF16)
        return lp

    def fused(x, qz, j, lp):
        ks = slice(j * ATT_K, (j + 1) * ATT_K)
        p = jnp.exp2(lax.dot_general(qz, k_ref[ks, :], _NT, preferred_element_type=F32))
        for t in range(tiles):
            lp = lp + p[:, t * LANES:(t + 1) * LANES]
        p_ref[x % 2, :, ks] = p.astype(BF16)
        return lp

    def mix(x, rho, j, acc):
        ks = slice(j * ATT_K, (j + 1) * ATT_K)
        w = p_ref[x % 2, :ATT_Q, ks] - rho * p_ref[x % 2, ATT_Q:, ks]
        return acc + jnp.dot(w, v_ref[ks, :], preferred_element_type=F32)

    def finish(step, x, acc, lp):
        l = jnp.sum(lp, axis=-1, keepdims=True)
        if acc is None:
            return (lam * l[:ATT_Q] / l[ATT_Q:]).astype(BF16)
        o = acc * (1.0 / l[:ATT_Q])
        ms = jnp.mean(o * o, axis=-1, keepdims=True)
        o = o * lax.rsqrt(ms + SUBLN_EPS) * sub_ref[...] * (1.0 - LAM_INIT)
        o_ref[block_rows(step, x), :] = o.astype(o_ref.dtype)

    def fast_step(step, carry):
        lps = [None] * ATT_STREAMS
        for t in range(ATT_STREAMS + 1):
            if t < ATT_STREAMS:
                qz = masked_q(step, t)
                lp = jnp.zeros((2 * ATT_Q, LANES), F32)
                for j in range(nk):
                    lp = fused(t, qz, j, lp)
                lps[t] = lp
            if t >= 1:
                x = t - 1
                rho = finish(step, x, None, lps[x])
                acc = jnp.zeros((ATT_Q, DA_V_DIM), F32)
                for j in range(nk):
                    acc = mix(x, rho, j, acc)
                finish(step, x, acc, lps[x])
        return carry

    def slow_step(step, carry):
        qz = [None] * ATT_STREAMS
        mp = [None] * ATT_STREAMS
        lp = [None] * ATT_STREAMS
        m = [None] * ATT_STREAMS
        rho = [None] * ATT_STREAMS
        acc = [None] * ATT_STREAMS
        for t in range(ATT_STREAMS + 2):
            xs, xe, xm = t, t - 1, t - 2
            if xs < ATT_STREAMS:
                qz[xs] = masked_q(step, xs)
                mp[xs] = jnp.full((2 * ATT_Q, LANES), -jnp.inf, F32)
            if 0 <= xe < ATT_STREAMS:
                m[xe] = jnp.max(mp[xe], axis=-1, keepdims=True)
                lp[xe] = jnp.zeros((2 * ATT_Q, LANES), F32)
            if 0 <= xm < ATT_STREAMS:
                rho[xm] = finish(step, xm, None, lp[xm])
                acc[xm] = jnp.zeros((ATT_Q, DA_V_DIM), F32)
            for j in range(nk):
                if xs < ATT_STREAMS:
                    mp[xs] = scores(xs, qz[xs], j, mp[xs])
                if 0 <= xe < ATT_STREAMS:
                    lp[xe] = exps(xe, m[xe], j, lp[xe])
                if 0 <= xm < ATT_STREAMS:
                    acc[xm] = mix(xm, rho[xm], j, acc[xm])
            if 0 <= xm < ATT_STREAMS:
                finish(step, xm, acc[xm], lp[xm])
        return carry

    steps = seq // (ATT_STREAMS * ATT_Q)

    @pl.when(safe)
    def _():
        lax.fori_loop(0, steps, fast_step, 0)

    @pl.when(jnp.logical_not(safe))
    def _():
        lax.fori_loop(0, steps, slow_step, 0)


def _attention(lamv, subln, qd, kd, vd, batch, seq):
    n = qd.shape[0]
    assert seq % (ATT_STREAMS * ATT_Q) == 0
    slab = pl.BlockSpec((seq, DA_V_DIM), lambda b, h: (b, h))
    return pl.pallas_call(
        _attn_kernel,
        grid=(batch, DA_HEADS),
        in_specs=[pl.BlockSpec((4, DA_HEAD_DIM), lambda b, h: (0, 0)),
                  pl.BlockSpec((1, DA_V_DIM), lambda b, h: (0, 0)),
                  slab, slab, slab],
        out_specs=slab,
        out_shape=jax.ShapeDtypeStruct((n, DA_WIDTH), BF16),
        scratch_shapes=[pltpu.VMEM((2, 2 * ATT_Q, seq), F32),
                        pltpu.VMEM((2, 2 * ATT_Q, seq), BF16)],
        compiler_params=pltpu.CompilerParams(
            dimension_semantics=("parallel", "parallel"), vmem_limit_bytes=VMEM_LIMIT),
        name="diffattn",
    )(lamv, subln, qd, kd, vd)


def _cumsum_rows(x, rev):
    c = x.shape[0]
    row = lax.broadcasted_iota(jnp.int32, x.shape, 0)
    sh = 1
    while sh < c:
        if rev:
            x = x + jnp.where(row < c - sh, pltpu.roll(x, c - sh, 0), 0.0)
        else:
            x = x + jnp.where(row >= sh, pltpu.roll(x, sh, 0), 0.0)
        sh *= 2
    return x


def _block_row(g, half, rev):
    c, w = g.shape
    idx = half if rev else half - 1
    blk = 2 * half
    if blk >= SUBLANES:
        g3 = g.reshape(c // blk, blk, w)
        return jnp.broadcast_to(g3[:, idx:idx + 1, :], g3.shape).reshape(c, w)
    g3 = g.reshape(c // SUBLANES, SUBLANES, w)
    sub = lax.broadcasted_iota(jnp.int32, g3.shape, 1)
    out = None
    for b in reversed(range(SUBLANES // blk)):
        piece = jnp.broadcast_to(g3[:, b * blk + idx:b * blk + idx + 1, :], g3.shape)
        out = piece if out is None else jnp.where(sub < (b + 1) * blk, piece, out)
    return out.reshape(c, w)


def _gla_chunk(q, f, v, st_ref, pair_level, rev):
    c = q.shape[0]
    qf = q.astype(F32)
    vf = v.astype(F32)
    kk = 1.0 - f
    g = _cumsum_rows(jnp.log2(f), rev)
    row = lax.broadcasted_iota(jnp.int32, (c, 1), 0)

    st = st_ref[...]
    o = lax.dot_general((qf * jnp.exp2(g)).astype(BF16), st.astype(BF16), _NT,
                        preferred_element_type=F32)
    o = o + jnp.sum(qf * kk, axis=-1, keepdims=True) * vf

    a = jnp.zeros((c, c), F32)
    half = 1
    lvl = 0
    while half < c:
        second = (row & half) != 0
        tgt = jnp.logical_not(second) if rev else second
        if half == 1:
            d = jnp.where(tgt, f, 1.0)
        else:
            d = jnp.exp2((g - _block_row(g, half, rev)) * jnp.where(tgt, 1.0, -1.0))
        xf = jnp.where(tgt, qf, kk) * d
        x = xf.astype(BF16)
        if half >= SUBLANES:
            first = 0 if rev else half
            pick = lambda m: jnp.concatenate(
                [m[b + first:b + first + half] for b in range(0, c, 2 * half)], axis=0)
            al = lax.dot_general(pick(xf).astype(BF16), x, _NT, preferred_element_type=F32)
            upd = jnp.where(pick(pair_level) == lvl, al, pick(a))
            rows = []
            for i, b in enumerate(range(0, c, 2 * half)):
                other = a[b + half - first:b + 2 * half - first]
                mine = upd[i * half:(i + 1) * half]
                rows += [mine, other] if rev else [other, mine]
            a = jnp.concatenate(rows, axis=0)
        else:
            al = lax.dot_general(x, x, _NT, preferred_element_type=F32)
            a = jnp.where(pair_level == lvl, al, a)
        half *= 2
        lvl += 1
    o = o + jnp.dot(a.astype(BF16), v, preferred_element_type=F32)

    g_tot = g[0:1, :] if rev else g[c - 1:c, :]
    ks = (kk * jnp.exp2(g_tot - g)).astype(BF16)
    st_ref[...] = st * jnp.exp2(g_tot) + lax.dot_general(v, ks, _TN, preferred_element_type=F32)
    return o


def _hgrn_kernel(lv_ref, qf_ref, ff_ref, vf_ref, qb_ref, fb_ref, vb_ref, of_ref, ob_ref, st_ref):
    @pl.when(pl.program_id(1) == 0)
    def _():
        st_ref[...] = jnp.zeros_like(st_ref)

    lv_fwd = lv_ref[0]
    lv_bwd = lv_ref[1]
    for r in range(HG_ROWS):
        for h in range(HG_HEADS):
            sl = slice(h * HG_HEAD_DIM, (h + 1) * HG_HEAD_DIM)
            of_ref[r, :, sl] = _gla_chunk(qf_ref[r, :, sl], ff_ref[r, :, sl], vf_ref[r, :, sl],
                                          st_ref.at[r, h], lv_fwd, False)
            ob_ref[r, :, sl] = _gla_chunk(qb_ref[r, :, sl], fb_ref[r, :, sl], vb_ref[r, :, sl],
                                          st_ref.at[r, HG_HEADS + h], lv_bwd, True)


def _pair_levels(c):
    t = jnp.arange(c, dtype=jnp.int32)[:, None]
    s = jnp.arange(c, dtype=jnp.int32)[None, :]
    x = t ^ s
    lv = jnp.full((c, c), -1, jnp.int32)
    for i in range(c.bit_length() - 1):
        lv = jnp.where((x >> i) == 1, i, lv)
    return jnp.stack([jnp.where(t > s, lv, -1), jnp.where(t < s, lv, -1)])


def _hgrn(qh, ff, fb, ih, batch, seq):
    c = HG_CHUNK
    assert batch % HG_ROWS == 0
    nc = seq // c
    fwd = pl.BlockSpec((HG_ROWS, c, HG_WIDTH), lambda b, j: (b, j, 0))
    bwd = pl.BlockSpec((HG_ROWS, c, HG_WIDTH), lambda b, j: (b, nc - 1 - j, 0))
    qh, ff, fb, ih = (a.reshape(batch, seq, HG_WIDTH) for a in (qh, ff, fb, ih))
    return pl.pallas_call(
        _hgrn_kernel,
        grid=(batch // HG_ROWS, nc),
        in_specs=[pl.BlockSpec((2, c, c), lambda b, j: (0, 0, 0)), fwd, fwd, fwd, bwd, bwd, bwd],
        out_specs=[fwd, bwd],
        out_shape=[jax.ShapeDtypeStruct((batch, seq, HG_WIDTH), F32)] * 2,
        scratch_shapes=[pltpu.VMEM((HG_ROWS, 2 * HG_HEADS, HG_HEAD_DIM, HG_HEAD_DIM), F32)],
        compiler_params=pltpu.CompilerParams(dimension_semantics=("parallel", "arbitrary"),
                                             vmem_limit_bytes=VMEM_LIMIT),
        name="hgrn2",
    )(_pair_levels(c), qh, ff, ih, qh, fb, ih)


def _merge_mlp_kernel(x_ref, of_ref, ob_ref, gh_ref, oa_ref, sgh_ref, sgd_ref,
                      hgn_ref, n2_ref, fn_ref, whg_ref, wda_ref, wout_ref, w1_ref, w2_ref, y_ref):
    o = of_ref[...] + ob_ref[...]
    parts = []
    for h in range(HG_HEADS):
        sl = slice(h * HG_HEAD_DIM, (h + 1) * HG_HEAD_DIM)
        oh = o[:, sl]
        ms = jnp.mean(oh * oh, axis=-1, keepdims=True)
        parts.append(oh * lax.rsqrt(ms + NORM_EPS) * hgn_ref[:, sl])
    o = jnp.concatenate(parts, axis=1) * gh_ref[...].astype(F32)
    y_hg = jnp.dot(o.astype(BF16), whg_ref[...], preferred_element_type=F32)
    y_da = jnp.dot(oa_ref[...], wda_ref[...], preferred_element_type=F32)
    m = sgh_ref[...].astype(F32) * y_hg + sgd_ref[...].astype(F32) * y_da
    x1 = x_ref[...] + jnp.dot(m.astype(BF16), wout_ref[...], preferred_element_type=F32)

    ms = jnp.mean(x1 * x1, axis=-1, keepdims=True)
    h2 = (x1 * lax.rsqrt(ms + NORM_EPS) * n2_ref[...]).astype(BF16)
    acc = x1
    for j in range(D_FF // MLP_COLS):
        cs = slice(j * MLP_COLS, (j + 1) * MLP_COLS)
        mid = jnp.maximum(jnp.dot(h2, w1_ref[:, cs], preferred_element_type=F32), 0.0)
        acc = acc + jnp.dot((mid * mid).astype(BF16), w2_ref[cs, :], preferred_element_type=F32)
    ms = jnp.mean(acc * acc, axis=-1, keepdims=True)
    y_ref[...] = acc * lax.rsqrt(ms + NORM_EPS) * fn_ref[...]


def _merge_mlp(x2, of, ob, gh, oa, sgh, sgd, hgn, n2, fn, whg, wda, wout, w1, w2):
    n = x2.shape[0]
    tm = MLP_ROWS
    row = lambda w: pl.BlockSpec((tm, w), lambda i: (i, 0))
    return pl.pallas_call(
        _merge_mlp_kernel,
        grid=(n // tm,),
        in_specs=[row(D_MODEL), row(HG_WIDTH), row(HG_WIDTH), row(HG_WIDTH), row(DA_WIDTH),
                  row(D_MODEL), row(D_MODEL),
                  _const_spec((1, HG_WIDTH)), _const_spec((1, D_MODEL)), _const_spec((1, D_MODEL)),
                  _const_spec((HG_WIDTH, D_MODEL)), _const_spec((DA_WIDTH, D_MODEL)),
                  _const_spec((D_MODEL, D_MODEL)), _const_spec((D_MODEL, D_FF)),
                  _const_spec((D_FF, D_MODEL))],
        out_specs=row(D_MODEL),
        out_shape=jax.ShapeDtypeStruct((n, D_MODEL), F32),
        compiler_params=pltpu.CompilerParams(dimension_semantics=("parallel",),
                                             vmem_limit_bytes=VMEM_LIMIT),
        name="merge_mlp",
    )(x2, of, ob, gh, oa, sgh, sgd, hgn, n2, fn, whg, wda, wout, w1, w2)


def _rotary_table(seq):
    pos = jnp.arange(seq, dtype=F32)
    inv_freq = ROPE_THETA ** (-jnp.arange(0, ROT_DIM, 2, dtype=F32) / ROT_DIM)
    ang = pos[:, None] * inv_freq[None, :]
    cos, sin = jnp.cos(ang), jnp.sin(ang)
    pad = jnp.zeros((seq, DA_HEAD_DIM - ROT_DIM), F32)
    zeros = jnp.zeros_like(sin)
    cos64 = jnp.concatenate([cos, cos, pad + 1.0], axis=1)
    lo64 = jnp.concatenate([zeros, sin, pad], axis=1)
    hi64 = jnp.concatenate([-sin, zeros, pad], axis=1)
    return jnp.concatenate([cos64, cos64, lo64, lo64, hi64, hi64], axis=1)


def _trunk(x, p, rot):
    batch, seq, _ = x.shape
    assert seq % max(IN_ROWS, ATT_STREAMS * ATT_Q, ATT_K, HG_CHUNK, MLP_ROWS) == 0
    x2 = x.reshape(batch * seq, D_MODEL)
    qd, kd, vd, qh, ff, fb, ih, gh, sgh, sgd = _inproj(x2, p["n1"], p["w_in"], p["lbl"], rot, seq)
    oa = _attention(p["lamv"], p["subln"], qd, kd, vd, batch, seq)
    of, ob = (a.reshape(batch * seq, HG_WIDTH) for a in _hgrn(qh, ff, fb, ih, batch, seq))
    y = _merge_mlp(x2, of, ob, gh, oa, sgh, sgd, p["hgn"], p["n2"], p["fn"],
                   p["whg"], p["wda"], p["wout"], p["w1"], p["w2"])
    return y.reshape(batch, seq, D_MODEL)


def kernel(x_prompt, x_sample, norm1, w_in, hg_lb_logits, hg_norm, w_hg_branch, da_lambda_q1, da_lambda_k1, da_lambda_q2, da_lambda_k2, da_subln, w_da_branch, w_out, norm2, w_mlp_in, w_mlp_out, final_norm):
    assert norm1.shape[0] == 1 and hg_lb_logits.shape == (2, 2, HG_WIDTH)
    p = dict(
        n1=norm1.reshape(1, D_MODEL), w_in=w_in[0].astype(BF16),
        lbl=hg_lb_logits.reshape(4, HG_WIDTH),
        lamv=jnp.concatenate([da_lambda_q1, da_lambda_k1, da_lambda_q2, da_lambda_k2], axis=0),
        subln=da_subln.reshape(1, DA_V_DIM), hgn=hg_norm.reshape(1, HG_WIDTH),
        n2=norm2.reshape(1, D_MODEL), fn=final_norm.reshape(1, D_MODEL),
        whg=w_hg_branch[0].astype(BF16), wda=w_da_branch[0].astype(BF16), wout=w_out[0].astype(BF16),
        w1=w_mlp_in[0].astype(BF16), w2=w_mlp_out[0].astype(BF16),
    )
    rot = _rotary_table(max(x_prompt.shape[1], x_sample.shape[1]))
    return _trunk(x_prompt, p, rot), _trunk(x_sample, p, rot)
```

```python
import math

import jax
import jax.numpy as jnp
from jax import lax
from jax.experimental import pallas as pl
from jax.experimental.pallas import tpu as pltpu

F32 = jnp.float32
BF16 = jnp.bfloat16

D_MODEL = 1024
HG_HEADS = 4
HG_HEAD_DIM = 128
HG_WIDTH = HG_HEADS * HG_HEAD_DIM
HG_SCALE = HG_HEAD_DIM ** -0.5
DA_HEADS = 4
DA_HEAD_DIM = 64
DA_V_DIM = 2 * DA_HEAD_DIM
DA_WIDTH = DA_HEADS * DA_V_DIM
DA_SCALE = DA_HEAD_DIM ** -0.5
ROT_DIM = DA_HEAD_DIM // 4
ROT_HALF = ROT_DIM // 2
ROPE_THETA = 500000.0
D_FF = 4 * D_MODEL
NORM_EPS = 1e-6
SUBLN_EPS = 1e-5
IN_WIDTH = 3 * DA_WIDTH + 5 * HG_WIDTH + 2 * D_MODEL
LAM_INIT = 0.8 - 0.6 * math.exp(-0.3 * 0)

LANES = 128
SUBLANES = 8
VMEM_LIMIT = 56 * 1024 * 1024

SEG = 512
IN_ROWS = 1024
ATT_Q = 128
ATT_STREAMS = 4
ATT_K = 512
LOG2_E = math.log2(math.e)
ATT_SAFE_LOG2 = 48.0
HG_CHUNK = 128
HG_ROWS = 4
MLP_ROWS = 512
MLP_COLS = 1024

_NT = (((1,), (1,)), ((), ()))
_TN = (((0,), (0,)), ((), ()))


def _sigmoid(x):
    return 1.0 / (1.0 + jnp.exp(-x))


def _const_spec(shape):
    nd = len(shape)
    return pl.BlockSpec(shape, lambda *_: (0,) * nd, pipeline_mode=pl.Buffered(1))


def _inproj_kernel(x_ref, n1_ref, w_ref, lbl_ref, rot_ref,
                   qd_ref, kd_ref, vd_ref, qh_ref, ff_ref, fb_ref, ih_ref, gh_ref, sgh_ref, sgd_ref):
    x = x_ref[...]
    ms = jnp.mean(x * x, axis=-1, keepdims=True)
    h = (x * lax.rsqrt(ms + NORM_EPS) * n1_ref[...]).astype(BF16)

    def seg(j):
        return jnp.dot(h, w_ref[:, j * SEG:(j + 1) * SEG], preferred_element_type=F32)

    cos_t = rot_ref[:, 0:LANES]
    sin_lo = rot_ref[:, LANES:2 * LANES]
    sin_hi = rot_ref[:, 2 * LANES:3 * LANES]

    def rope_store(u, out_ref, scale):
        for p in range(SEG // LANES):
            a = u[:, p * LANES:(p + 1) * LANES]
            r = (a * cos_t + pltpu.roll(a, ROT_HALF, 1) * sin_lo
                 + pltpu.roll(a, LANES - ROT_HALF, 1) * sin_hi)
            if scale != 1.0:
                r = r * scale
            out_ref[:, p * LANES:(p + 1) * LANES] = r.astype(out_ref.dtype)

    rope_store(seg(0), qd_ref, DA_SCALE * LOG2_E)
    rope_store(seg(1), kd_ref, 1.0)
    vd_ref[...] = seg(2).astype(BF16)

    u = seg(3)
    qh_ref[...] = (u * _sigmoid(u) * HG_SCALE).astype(BF16)

    def lower_bound(d):
        l0 = lbl_ref[2 * d:2 * d + 1, :]
        l1 = lbl_ref[2 * d + 1:2 * d + 2, :]
        m = jnp.maximum(l0, l1)
        e0 = jnp.exp(l0 - m)
        e1 = jnp.exp(l1 - m)
        return e0 / (e0 + e1)

    lb = lower_bound(0)
    ff_ref[...] = lb + (1.0 - lb) * _sigmoid(seg(4))
    lb = lower_bound(1)
    fb_ref[...] = lb + (1.0 - lb) * _sigmoid(seg(5))
    ih_ref[...] = seg(6).astype(BF16)
    u = seg(7)
    gh_ref[...] = (u * _sigmoid(u)).astype(BF16)
    for p in range(D_MODEL // SEG):
        sgh_ref[:, p * SEG:(p + 1) * SEG] = _sigmoid(seg(8 + p)).astype(BF16)
        sgd_ref[:, p * SEG:(p + 1) * SEG] = _sigmoid(seg(10 + p)).astype(BF16)


def _inproj(x2, n1, w_in, lbl, rot, seq):
    n = x2.shape[0]
    tm = min(IN_ROWS, seq)
    nt = seq // tm
    row = lambda w: pl.BlockSpec((tm, w), lambda i: (i, 0))
    out_w = (DA_WIDTH,) * 3 + (HG_WIDTH,) * 5 + (D_MODEL,) * 2
    out_dt = (BF16, BF16, BF16, BF16, F32, F32, BF16, BF16, BF16, BF16)
    return pl.pallas_call(
        _inproj_kernel,
        grid=(n // tm,),
        in_specs=[row(D_MODEL), _const_spec((1, D_MODEL)), _const_spec((D_MODEL, IN_WIDTH)),
                  _const_spec((4, HG_WIDTH)),
                  pl.BlockSpec((tm, 3 * LANES), lambda i: (i % nt, 0))],
        out_specs=[row(w) for w in out_w],
        out_shape=[jax.ShapeDtypeStruct((n, w), dt) for w, dt in zip(out_w, out_dt)],
        compiler_params=pltpu.CompilerParams(dimension_semantics=("parallel",),
                                             vmem_limit_bytes=VMEM_LIMIT),
        name="inproj",
    )(x2, n1, w_in, lbl, rot)


def _attn_kernel(lam_ref, sub_ref, q_ref, k_ref, v_ref, o_ref, s_ref, p_ref):
    seq = k_ref.shape[0]
    nk = seq // ATT_K
    tiles = ATT_K // LANES

    def bound_check():
        d = lax.broadcasted_iota(jnp.int32, (DA_V_DIM, LANES), 0)
        c = lax.broadcasted_iota(jnp.int32, (DA_V_DIM, LANES), 1)
        pick = jnp.where(c == d // DA_HEAD_DIM, 1.0, 0.0).astype(BF16)

        def max_norm2(ref):
            a = ref[...].astype(F32)
            return jnp.max(jnp.dot((a * a).astype(BF16), pick, preferred_element_type=F32), axis=0, keepdims=True)

        b2 = max_norm2(q_ref) * max_norm2(k_ref)
        b2 = jnp.maximum(b2[:, 0:1], b2[:, 1:2])
        return b2[0, 0] <= 0.98 * ATT_SAFE_LOG2 * ATT_SAFE_LOG2

    safe = bound_check()

    lv = lam_ref[...]
    lam = (jnp.exp(jnp.sum(lv[0:1] * lv[1:2], axis=-1, keepdims=True))
           - jnp.exp(jnp.sum(lv[2:3] * lv[3:4], axis=-1, keepdims=True)) + LAM_INIT)
    lane = lax.broadcasted_iota(jnp.int32, (ATT_Q, DA_V_DIM), 1)

    def block_rows(step, x):
        return pl.ds(pl.multiple_of((step * ATT_STREAMS + x) * ATT_Q, ATT_Q), ATT_Q)

    def masked_q(step, x):
        q = q_ref[block_rows(step, x), :]
        zero = jnp.zeros_like(q)
        return jnp.concatenate([jnp.where(lane < DA_HEAD_DIM, q, zero),
                                jnp.where(lane >= DA_HEAD_DIM, q, zero)], axis=0)

    def scores(x, qz, j, mp):
        ks = slice(j * ATT_K, (j + 1) * ATT_K)
        s = lax.dot_general(qz, k_ref[ks, :], _NT, preferred_element_type=F32)
        s_ref[x % 2, :, ks] = s
        for t in range(tiles):
            mp = jnp.maximum(mp, s[:, t * LANES:(t + 1) * LANES])
        return mp

    def exps(x, m, j, lp):
        ks = slice(j * ATT_K, (j + 1) * ATT_K)
        p = jnp.exp2(s_ref[x % 2, :, ks] - m)
        for t in range(tiles):
            lp = lp + p[:, t * LANES:(t + 1) * LANES]
        p_ref[x % 2, :, ks] = p.astype(BF16)
        return lp

    def fused(x, qz, j, lp):
        ks = slice(j * ATT_K, (j + 1) * ATT_K)
        p = jnp.exp2(lax.dot_general(qz, k_ref[ks, :], _NT, preferred_element_type=F32))
        for t in range(tiles):
            lp = lp + p[:, t * LANES:(t + 1) * LANES]
        p_ref[x % 2, :, ks] = p.astype(BF16)
        return lp

    def mix(x, rho, j, acc):
        ks = slice(j * ATT_K, (j + 1) * ATT_K)
        w = p_ref[x % 2, :ATT_Q, ks] - rho * p_ref[x % 2, ATT_Q:, ks]
        return acc + jnp.dot(w, v_ref[ks, :], preferred_element_type=F32)

    def finish(step, x, acc, lp):
        l = jnp.sum(lp, axis=-1, keepdims=True)
        if acc is None:
            return (lam * l[:ATT_Q] / l[ATT_Q:]).astype(BF16)
        o = acc * (1.0 / l[:ATT_Q])
        ms = jnp.mean(o * o, axis=-1, keepdims=True)
        o = o * lax.rsqrt(ms + SUBLN_EPS) * sub_ref[...] * (1.0 - LAM_INIT)
        o_ref[block_rows(step, x), :] = o.astype(o_ref.dtype)

    def fast_step(step, carry):
        lps = [None] * ATT_STREAMS
        for t in range(ATT_STREAMS + 1):
            if t < ATT_STREAMS:
                qz = masked_q(step, t)
                lp = jnp.zeros((2 * ATT_Q, LANES), F32)
                for j in range(nk):
                    lp = fused(t, qz, j, lp)
                lps[t] = lp
            if t >= 1:
                x = t - 1
                rho = finish(step, x, None, lps[x])
                acc = jnp.zeros((ATT_Q, DA_V_DIM), F32)
                for j in range(nk):
                    acc = mix(x, rho, j, acc)
                finish(step, x, acc, lps[x])
        return carry

    def slow_step(step, carry):
        qz = [None] * ATT_STREAMS
        mp = [None] * ATT_STREAMS
        lp = [None] * ATT_STREAMS
        m = [None] * ATT_STREAMS
        rho = [None] * ATT_STREAMS
        acc = [None] * ATT_STREAMS
        for t in range(ATT_STREAMS + 2):
            xs, xe, xm = t, t - 1, t - 2
            if xs < ATT_STREAMS:
                qz[xs] = masked_q(step, xs)
                mp[xs] = jnp.full((2 * ATT_Q, LANES), -jnp.inf, F32)
            if 0 <= xe < ATT_STREAMS:
                m[xe] = jnp.max(mp[xe], axis=-1, keepdims=True)
                lp[xe] = jnp.zeros((2 * ATT_Q, LANES), F32)
            if 0 <= xm < ATT_STREAMS:
                rho[xm] = finish(step, xm, None, lp[xm])
                acc[xm] = jnp.zeros((ATT_Q, DA_V_DIM), F32)
            for j in range(nk):
                if xs < ATT_STREAMS:
                    mp[xs] = scores(xs, qz[xs], j, mp[xs])
                if 0 <= xe < ATT_STREAMS:
                    lp[xe] = exps(xe, m[xe], j, lp[xe])
                if 0 <= xm < ATT_STREAMS:
                    acc[xm] = mix(xm, rho[xm], j, acc[xm])
            if 0 <= xm < ATT_STREAMS:
                finish(step, xm, acc[xm], lp[xm])
        return carry

    steps = seq // (ATT_STREAMS * ATT_Q)

    @pl.when(safe)
    def _():
        lax.fori_loop(0, steps, fast_step, 0)

    @pl.when(jnp.logical_not(safe))
    def _():
        lax.fori_loop(0, steps, slow_step, 0)


def _attention(lamv, subln, qd, kd, vd, batch, seq):
    n = qd.shape[0]
    assert seq % (ATT_STREAMS * ATT_Q) == 0
    slab = pl.BlockSpec((seq, DA_V_DIM), lambda b, h: (b, h))
    return pl.pallas_call(
        _attn_kernel,
        grid=(batch, DA_HEADS),
        in_specs=[pl.BlockSpec((4, DA_HEAD_DIM), lambda b, h: (0, 0)),
                  pl.BlockSpec((1, DA_V_DIM), lambda b, h: (0, 0)),
                  slab, slab, slab],
        out_specs=slab,
        out_shape=jax.ShapeDtypeStruct((n, DA_WIDTH), BF16),
        scratch_shapes=[pltpu.VMEM((2, 2 * ATT_Q, seq), F32),
                        pltpu.VMEM((2, 2 * ATT_Q, seq), BF16)],
        compiler_params=pltpu.CompilerParams(
            dimension_semantics=("parallel", "parallel"), vmem_limit_bytes=VMEM_LIMIT),
        name="diffattn",
    )(lamv, subln, qd, kd, vd)


def _cumsum_rows(x, rev):
    c = x.shape[0]
    row = lax.broadcasted_iota(jnp.int32, x.shape, 0)
    sh = 1
    while sh < c:
        if rev:
            x = x + jnp.where(row < c - sh, pltpu.roll(x, c - sh, 0), 0.0)
        else:
            x = x + jnp.where(row >= sh, pltpu.roll(x, sh, 0), 0.0)
        sh *= 2
    return x


def _block_row(g, half, rev):
    c, w = g.shape
    idx = half if rev else half - 1
    blk = 2 * half
    if blk >= SUBLANES:
        g3 = g.reshape(c // blk, blk, w)
        return jnp.broadcast_to(g3[:, idx:idx + 1, :], g3.shape).reshape(c, w)
    g3 = g.reshape(c // SUBLANES, SUBLANES, w)
    sub = lax.broadcasted_iota(jnp.int32, g3.shape, 1)
    out = None
    for b in reversed(range(SUBLANES // blk)):
        piece = jnp.broadcast_to(g3[:, b * blk + idx:b * blk + idx + 1, :], g3.shape)
        out = piece if out is None else jnp.where(sub < (b + 1) * blk, piece, out)
    return out.reshape(c, w)


def _gla_chunk(q, f, v, st_ref, pair_level, rev):
    c = q.shape[0]
    qf = q.astype(F32)
    vf = v.astype(F32)
    kk = 1.0 - f
    g = _cumsum_rows(jnp.log2(f), rev)
    row = lax.broadcasted_iota(jnp.int32, (c, 1), 0)

    st = st_ref[...]
    o = lax.dot_general((qf * jnp.exp2(g)).astype(BF16), st.astype(BF16), _NT,
                        preferred_element_type=F32)
    o = o + jnp.sum(qf * kk, axis=-1, keepdims=True) * vf

    a = jnp.zeros((c, c), F32)
    half = 1
    lvl = 0
    while half < c:
        second = (row & half) != 0
        tgt = jnp.logical_not(second) if rev else second
        if half == 1:
            d = jnp.where(tgt, f, 1.0)
        else:
            d = jnp.exp2((g - _block_row(g, half, rev)) * jnp.where(tgt, 1.0, -1.0))
        xf = jnp.where(tgt, qf, kk) * d
        x = xf.astype(BF16)
        if half >= SUBLANES:
            first = 0 if rev else half
            pick = lambda m: jnp.concatenate(
                [m[b + first:b + first + half] for b in range(0, c, 2 * half)], axis=0)
            al = lax.dot_general(pick(xf).astype(BF16), x, _NT, preferred_element_type=F32)
            upd = jnp.where(pick(pair_level) == lvl, al, pick(a))
            rows = []
            for i, b in enumerate(range(0, c, 2 * half)):
                other = a[b + half - first:b + 2 * half - first]
                mine = upd[i * half:(i + 1) * half]
                rows += [mine, other] if rev else [other, mine]
            a = jnp.concatenate(rows, axis=0)
        else:
            al = lax.dot_general(x, x, _NT, preferred_element_type=F32)
            a = jnp.where(pair_level == lvl, al, a)
        half *= 2
        lvl += 1
    o = o + jnp.dot(a.astype(BF16), v, preferred_element_type=F32)

    g_tot = g[0:1, :] if rev else g[c - 1:c, :]
    ks = (kk * jnp.exp2(g_tot - g)).astype(BF16)
    st_ref[...] = st * jnp.exp2(g_tot) + lax.dot_general(v, ks, _TN, preferred_element_type=F32)
    return o


def _hgrn_kernel(lv_ref, qf_ref, ff_ref, vf_ref, qb_ref, fb_ref, vb_ref, of_ref, ob_ref, st_ref):
    @pl.when(pl.program_id(1) == 0)
    def _():
        st_ref[...] = jnp.zeros_like(st_ref)

    lv_fwd = lv_ref[0]
    lv_bwd = lv_ref[1]
    for r in range(HG_ROWS):
        for h in range(HG_HEADS):
            sl = slice(h * HG_HEAD_DIM, (h + 1) * HG_HEAD_DIM)
            of_ref[r, :, sl] = _gla_chunk(qf_ref[r, :, sl], ff_ref[r, :, sl], vf_ref[r, :, sl],
                                          st_ref.at[r, h], lv_fwd, False)
            ob_ref[r, :, sl] = _gla_chunk(qb_ref[r, :, sl], fb_ref[r, :, sl], vb_ref[r, :, sl],
                                          st_ref.at[r, HG_HEADS + h], lv_bwd, True)


def _pair_levels(c):
    t = jnp.arange(c, dtype=jnp.int32)[:, None]
    s = jnp.arange(c, dtype=jnp.int32)[None, :]
    x = t ^ s
    lv = jnp.full((c, c), -1, jnp.int32)
    for i in range(c.bit_length() - 1):
        lv = jnp.where((x >> i) == 1, i, lv)
    return jnp.stack([jnp.where(t > s, lv, -1), jnp.where(t < s, lv, -1)])


def _hgrn(qh, ff, fb, ih, batch, seq):
    c = HG_CHUNK
    assert batch % HG_ROWS == 0
    nc = seq // c
    fwd = pl.BlockSpec((HG_ROWS, c, HG_WIDTH), lambda b, j: (b, j, 0))
    bwd = pl.BlockSpec((HG_ROWS, c, HG_WIDTH), lambda b, j: (b, nc - 1 - j, 0))
    qh, ff, fb, ih = (a.reshape(batch, seq, HG_WIDTH) for a in (qh, ff, fb, ih))
    return pl.pallas_call(
        _hgrn_kernel,
        grid=(batch // HG_ROWS, nc),
        in_specs=[pl.BlockSpec((2, c, c), lambda b, j: (0, 0, 0)), fwd, fwd, fwd, bwd, bwd, bwd],
        out_specs=[fwd, bwd],
        out_shape=[jax.ShapeDtypeStruct((batch, seq, HG_WIDTH), F32)] * 2,
        scratch_shapes=[pltpu.VMEM((HG_ROWS, 2 * HG_HEADS, HG_HEAD_DIM, HG_HEAD_DIM), F32)],
        compiler_params=pltpu.CompilerParams(dimension_semantics=("parallel", "arbitrary"),
                                             vmem_limit_bytes=VMEM_LIMIT),
        name="hgrn2",
    )(_pair_levels(c), qh, ff, ih, qh, fb, ih)


def _merge_mlp_kernel(x_ref, of_ref, ob_ref, gh_ref, oa_ref, sgh_ref, sgd_ref,
                      hgn_ref, n2_ref, fn_ref, whg_ref, wda_ref, wout_ref, w1_ref, w2_ref, y_ref):
    o = of_ref[...] + ob_ref[...]
    parts = []
    for h in range(HG_HEADS):
        sl = slice(h * HG_HEAD_DIM, (h + 1) * HG_HEAD_DIM)
        oh = o[:, sl]
        ms = jnp.mean(oh * oh, axis=-1, keepdims=True)
        parts.append(oh * lax.rsqrt(ms + NORM_EPS) * hgn_ref[:, sl])
    o = jnp.concatenate(parts, axis=1) * gh_ref[...].astype(F32)
    y_hg = jnp.dot(o.astype(BF16), whg_ref[...], preferred_element_type=F32)
    y_da = jnp.dot(oa_ref[...], wda_ref[...], preferred_element_type=F32)
    m = sgh_ref[...].astype(F32) * y_hg + sgd_ref[...].astype(F32) * y_da
    x1 = x_ref[...] + jnp.dot(m.astype(BF16), wout_ref[...], preferred_element_type=F32)

    ms = jnp.mean(x1 * x1, axis=-1, keepdims=True)
    h2 = (x1 * lax.rsqrt(ms + NORM_EPS) * n2_ref[...]).astype(BF16)
    acc = x1
    for j in range(D_FF // MLP_COLS):
        cs = slice(j * MLP_COLS, (j + 1) * MLP_COLS)
        mid = jnp.maximum(jnp.dot(h2, w1_ref[:, cs], preferred_element_type=F32), 0.0)
        acc = acc + jnp.dot((mid * mid).astype(BF16), w2_ref[cs, :], preferred_element_type=F32)
    ms = jnp.mean(acc * acc, axis=-1, keepdims=True)
    y_ref[...] = acc * lax.rsqrt(ms + NORM_EPS) * fn_ref[...]


def _merge_mlp(x2, of, ob, gh, oa, sgh, sgd, hgn, n2, fn, whg, wda, wout, w1, w2):
    n = x2.shape[0]
    tm = MLP_ROWS
    row = lambda w: pl.BlockSpec((tm, w), lambda i: (i, 0))
    return pl.pallas_call(
        _merge_mlp_kernel,
        grid=(n // tm,),
        in_specs=[row(D_MODEL), row(HG_WIDTH), row(HG_WIDTH), row(HG_WIDTH), row(DA_WIDTH),
                  row(D_MODEL), row(D_MODEL),
                  _const_spec((1, HG_WIDTH)), _const_spec((1, D_MODEL)), _const_spec((1, D_MODEL)),
                  _const_spec((HG_WIDTH, D_MODEL)), _const_spec((DA_WIDTH, D_MODEL)),
                  _const_spec((D_MODEL, D_MODEL)), _const_spec((D_MODEL, D_FF)),
                  _const_spec((D_FF, D_MODEL))],
        out_specs=row(D_MODEL),
        out_shape=jax.ShapeDtypeStruct((n, D_MODEL), F32),
        compiler_params=pltpu.CompilerParams(dimension_semantics=("parallel",),
                                             vmem_limit_bytes=VMEM_LIMIT),
        name="merge_mlp",
    )(x2, of, ob, gh, oa, sgh, sgd, hgn, n2, fn, whg, wda, wout, w1, w2)


def _rotary_table(seq):
    pos = jnp.arange(seq, dtype=F32)
    inv_freq = ROPE_THETA ** (-jnp.arange(0, ROT_DIM, 2, dtype=F32) / ROT_DIM)
    ang = pos[:, None] * inv_freq[None, :]
    cs = jnp.concatenate([jnp.cos(ang), jnp.sin(ang)], axis=1)
    lane = jnp.arange(3 * LANES)
    table, c = lane // LANES, lane % DA_HEAD_DIM
    src_row = jnp.arange(2 * ROT_HALF)[:, None]
    is_cos = (table == 0) & (c < ROT_DIM) & (src_row == c % ROT_HALF)
    is_lo = (table == 1) & (c >= ROT_HALF) & (c < ROT_DIM) & (src_row == ROT_HALF + c % ROT_HALF)
    is_hi = (table == 2) & (c < ROT_HALF) & (src_row == ROT_HALF + c)
    sel = jnp.where(is_cos | is_lo, 1.0, 0.0) - jnp.where(is_hi, 1.0, 0.0)
    ones = jnp.where((table == 0) & (c >= ROT_DIM), 1.0, 0.0)[None, :]
    return jnp.dot(cs, sel.astype(F32), precision=lax.Precision.HIGHEST) + ones


def _trunk(x, p, rot):
    batch, seq, _ = x.shape
    assert seq % max(IN_ROWS, ATT_STREAMS * ATT_Q, ATT_K, HG_CHUNK, MLP_ROWS) == 0
    x2 = x.reshape(batch * seq, D_MODEL)
    qd, kd, vd, qh, ff, fb, ih, gh, sgh, sgd = _inproj(x2, p["n1"], p["w_in"], p["lbl"], rot, seq)
    oa = _attention(p["lamv"], p["subln"], qd, kd, vd, batch, seq)
    of, ob = (a.reshape(batch * seq, HG_WIDTH) for a in _hgrn(qh, ff, fb, ih, batch, seq))
    y = _merge_mlp(x2, of, ob, gh, oa, sgh, sgd, p["hgn"], p["n2"], p["fn"],
                   p["whg"], p["wda"], p["wout"], p["w1"], p["w2"])
    return y.reshape(batch, seq, D_MODEL)


def kernel(x_prompt, x_sample, norm1, w_in, hg_lb_logits, hg_norm, w_hg_branch, da_lambda_q1, da_lambda_k1, da_lambda_q2, da_lambda_k2, da_subln, w_da_branch, w_out, norm2, w_mlp_in, w_mlp_out, final_norm):
    assert norm1.shape[0] == 1 and hg_lb_logits.shape == (2, 2, HG_WIDTH)
    p = dict(
        n1=norm1.reshape(1, D_MODEL), w_in=w_in[0].astype(BF16),
        lbl=hg_lb_logits.reshape(4, HG_WIDTH),
        lamv=jnp.concatenate([da_lambda_q1, da_lambda_k1, da_lambda_q2, da_lambda_k2], axis=0),
        subln=da_subln.reshape(1, DA_V_DIM), hgn=hg_norm.reshape(1, HG_WIDTH),
        n2=norm2.reshape(1, D_MODEL), fn=final_norm.reshape(1, D_MODEL),
        whg=w_hg_branch[0].astype(BF16), wda=w_da_branch[0].astype(BF16), wout=w_out[0].astype(BF16),
        w1=w_mlp_in[0].astype(BF16), w2=w_mlp_out[0].astype(BF16),
    )
    rot = _rotary_table(max(x_prompt.shape[1], x_sample.shape[1]))
    return _trunk(x_prompt, p, rot), _trunk(x_sample, p, rot)
```

```python
import math

import jax
import jax.numpy as jnp
from jax import lax
from jax.experimental import pallas as pl
from jax.experimental.pallas import tpu as pltpu

F32 = jnp.float32
BF16 = jnp.bfloat16

D_MODEL = 1024
HG_HEADS = 4
HG_HEAD_DIM = 128
HG_WIDTH = HG_HEADS * HG_HEAD_DIM
HG_SCALE = HG_HEAD_DIM ** -0.5
DA_HEADS = 4
DA_HEAD_DIM = 64
DA_V_DIM = 2 * DA_HEAD_DIM
DA_WIDTH = DA_HEADS * DA_V_DIM
DA_SCALE = DA_HEAD_DIM ** -0.5
ROT_DIM = DA_HEAD_DIM // 4
ROT_HALF = ROT_DIM // 2
ROPE_THETA = 500000.0
D_FF = 4 * D_MODEL
NORM_EPS = 1e-6
SUBLN_EPS = 1e-5
IN_WIDTH = 3 * DA_WIDTH + 5 * HG_WIDTH + 2 * D_MODEL
LAM_INIT = 0.8 - 0.6 * math.exp(-0.3 * 0)

LANES = 128
SUBLANES = 8
VMEM_LIMIT = 56 * 1024 * 1024

SEG = 512
IN_ROWS = 1024
ATT_Q = 128
ATT_STREAMS = 4
ATT_K = 512
LOG2_E = math.log2(math.e)
ATT_SAFE_LOG2 = 48.0
ATT_SAFE_LAM = 1024.0
HG_CHUNK = 128
HG_ROWS = 4
MLP_ROWS = 512
MLP_COLS = 1024

_NT = (((1,), (1,)), ((), ()))
_TN = (((0,), (0,)), ((), ()))


def _sigmoid(x):
    return 1.0 / (1.0 + jnp.exp(-x))


def _const_spec(shape):
    nd = len(shape)
    return pl.BlockSpec(shape, lambda *_: (0,) * nd, pipeline_mode=pl.Buffered(1))


def _inproj_kernel(x_ref, n1_ref, w_ref, lbl_ref, rot_ref,
                   qd_ref, kd_ref, vd_ref, qh_ref, ff_ref, fb_ref, ih_ref, gh_ref, sgh_ref, sgd_ref):
    x = x_ref[...]
    ms = jnp.mean(x * x, axis=-1, keepdims=True)
    h = (x * lax.rsqrt(ms + NORM_EPS) * n1_ref[...]).astype(BF16)

    def seg(j):
        return jnp.dot(h, w_ref[:, j * SEG:(j + 1) * SEG], preferred_element_type=F32)

    cos_t = rot_ref[:, 0:LANES]
    sin_lo = rot_ref[:, LANES:2 * LANES]
    sin_hi = rot_ref[:, 2 * LANES:3 * LANES]

    def rope_store(u, out_ref, scale):
        for p in range(SEG // LANES):
            a = u[:, p * LANES:(p + 1) * LANES]
            r = (a * cos_t + pltpu.roll(a, ROT_HALF, 1) * sin_lo
                 + pltpu.roll(a, LANES - ROT_HALF, 1) * sin_hi)
            if scale != 1.0:
                r = r * scale
            out_ref[:, p * LANES:(p + 1) * LANES] = r.astype(out_ref.dtype)

    rope_store(seg(0), qd_ref, DA_SCALE * LOG2_E)
    rope_store(seg(1), kd_ref, 1.0)
    vd_ref[...] = seg(2).astype(BF16)

    u = seg(3)
    qh_ref[...] = (u * _sigmoid(u) * HG_SCALE).astype(BF16)

    def lower_bound(d):
        l0 = lbl_ref[2 * d:2 * d + 1, :]
        l1 = lbl_ref[2 * d + 1:2 * d + 2, :]
        m = jnp.maximum(l0, l1)
        e0 = jnp.exp(l0 - m)
        e1 = jnp.exp(l1 - m)
        return e0 / (e0 + e1)

    lb = lower_bound(0)
    ff_ref[...] = lb + (1.0 - lb) * _sigmoid(seg(4))
    lb = lower_bound(1)
    fb_ref[...] = lb + (1.0 - lb) * _sigmoid(seg(5))
    ih_ref[...] = seg(6).astype(BF16)
    u = seg(7)
    gh_ref[...] = (u * _sigmoid(u)).astype(BF16)
    for p in range(D_MODEL // SEG):
        sgh_ref[:, p * SEG:(p + 1) * SEG] = _sigmoid(seg(8 + p)).astype(BF16)
        sgd_ref[:, p * SEG:(p + 1) * SEG] = _sigmoid(seg(10 + p)).astype(BF16)


def _inproj(x2, n1, w_in, lbl, rot, seq):
    n = x2.shape[0]
    tm = min(IN_ROWS, seq)
    nt = seq // tm
    row = lambda w: pl.BlockSpec((tm, w), lambda i: (i, 0))
    out_w = (DA_WIDTH,) * 3 + (HG_WIDTH,) * 5 + (D_MODEL,) * 2
    out_dt = (BF16, BF16, BF16, BF16, F32, F32, BF16, BF16, BF16, BF16)
    return pl.pallas_call(
        _inproj_kernel,
        grid=(n // tm,),
        in_specs=[row(D_MODEL), _const_spec((1, D_MODEL)), _const_spec((D_MODEL, IN_WIDTH)),
                  _const_spec((4, HG_WIDTH)),
                  pl.BlockSpec((tm, 3 * LANES), lambda i: (i % nt, 0))],
        out_specs=[row(w) for w in out_w],
        out_shape=[jax.ShapeDtypeStruct((n, w), dt) for w, dt in zip(out_w, out_dt)],
        compiler_params=pltpu.CompilerParams(dimension_semantics=("parallel",),
                                             vmem_limit_bytes=VMEM_LIMIT),
        name="inproj",
    )(x2, n1, w_in, lbl, rot)


def _attn_kernel(lam_ref, sub_ref, q_ref, k_ref, v_ref, o_ref, s_ref, p_ref):
    seq = k_ref.shape[0]
    nk = seq // ATT_K
    tiles = ATT_K // LANES

    def bound_check():
        d = lax.broadcasted_iota(jnp.int32, (DA_V_DIM, LANES), 0)
        c = lax.broadcasted_iota(jnp.int32, (DA_V_DIM, LANES), 1)
        pick = jnp.where(c == d // DA_HEAD_DIM, 1.0, 0.0).astype(BF16)

        def max_norm2(ref):
            a = ref[...].astype(F32)
            return jnp.max(jnp.dot((a * a).astype(BF16), pick, preferred_element_type=F32), axis=0, keepdims=True)

        b2 = max_norm2(q_ref) * max_norm2(k_ref)
        b2 = jnp.maximum(b2[:, 0:1], b2[:, 1:2])
        return b2[0, 0] <= 0.98 * ATT_SAFE_LOG2 * ATT_SAFE_LOG2

    lv = lam_ref[...]
    lam = (jnp.exp(jnp.sum(lv[0:1] * lv[1:2], axis=-1, keepdims=True))
           - jnp.exp(jnp.sum(lv[2:3] * lv[3:4], axis=-1, keepdims=True)) + LAM_INIT)
    safe = jnp.logical_and(bound_check(), jnp.abs(lam[0, 0]) <= ATT_SAFE_LAM)
    lane = lax.broadcasted_iota(jnp.int32, (ATT_Q, DA_V_DIM), 1)

    def block_rows(step, x):
        return pl.ds(pl.multiple_of((step * ATT_STREAMS + x) * ATT_Q, ATT_Q), ATT_Q)

    def masked_q(step, x):
        q = q_ref[block_rows(step, x), :]
        zero = jnp.zeros_like(q)
        return jnp.concatenate([jnp.where(lane < DA_HEAD_DIM, q, zero),
                                jnp.where(lane >= DA_HEAD_DIM, q, zero)], axis=0)

    def scores(x, qz, j, mp):
        ks = slice(j * ATT_K, (j + 1) * ATT_K)
        s = lax.dot_general(qz, k_ref[ks, :], _NT, preferred_element_type=F32)
        s_ref[x % 2, :, ks] = s
        for t in range(tiles):
            mp = jnp.maximum(mp, s[:, t * LANES:(t + 1) * LANES])
        return mp

    def exps(x, m, j, lp):
        ks = slice(j * ATT_K, (j + 1) * ATT_K)
        p = jnp.exp2(s_ref[x % 2, :, ks] - m)
        for t in range(tiles):
            lp = lp + p[:, t * LANES:(t + 1) * LANES]
        p_ref[x % 2, :, ks] = p.astype(BF16)
        return lp

    def fused(x, qz, j, lp):
        ks = slice(j * ATT_K, (j + 1) * ATT_K)
        p = jnp.exp2(lax.dot_general(qz, k_ref[ks, :], _NT, preferred_element_type=F32))
        for t in range(tiles):
            lp = lp + p[:, t * LANES:(t + 1) * LANES]
        p_ref[x % 2, :, ks] = p.astype(BF16)
        return lp

    def mix(x, rho, j, acc):
        ks = slice(j * ATT_K, (j + 1) * ATT_K)
        w = p_ref[x % 2, :ATT_Q, ks] - rho * p_ref[x % 2, ATT_Q:, ks]
        return acc + jnp.dot(w, v_ref[ks, :], preferred_element_type=F32)

    def finish(step, x, acc, lp):
        l = jnp.sum(lp, axis=-1, keepdims=True)
        if acc is None:
            return (lam * l[:ATT_Q] / l[ATT_Q:]).astype(BF16)
        o = acc * (1.0 / l[:ATT_Q])
        ms = jnp.mean(o * o, axis=-1, keepdims=True)
        o = o * lax.rsqrt(ms + SUBLN_EPS) * sub_ref[...] * (1.0 - LAM_INIT)
        o_ref[block_rows(step, x), :] = o.astype(o_ref.dtype)

    def fast_step(step, carry):
        lps = [None] * ATT_STREAMS
        for t in range(ATT_STREAMS + 1):
            if t < ATT_STREAMS:
                qz = masked_q(step, t)
                lp = jnp.zeros((2 * ATT_Q, LANES), F32)
                for j in range(nk):
                    lp = fused(t, qz, j, lp)
                lps[t] = lp
            if t >= 1:
                x = t - 1
                rho = finish(step, x, None, lps[x])
                acc = jnp.zeros((ATT_Q, DA_V_DIM), F32)
                for j in range(nk):
                    acc = mix(x, rho, j, acc)
                finish(step, x, acc, lps[x])
        return carry

    def slow_step(step, carry):
        qz = [None] * ATT_STREAMS
        mp = [None] * ATT_STREAMS
        lp = [None] * ATT_STREAMS
        m = [None] * ATT_STREAMS
        rho = [None] * ATT_STREAMS
        acc = [None] * ATT_STREAMS
        for t in range(ATT_STREAMS + 2):
            xs, xe, xm = t, t - 1, t - 2
            if xs < ATT_STREAMS:
                qz[xs] = masked_q(step, xs)
                mp[xs] = jnp.full((2 * ATT_Q, LANES), -jnp.inf, F32)
            if 0 <= xe < ATT_STREAMS:
                m[xe] = jnp.max(mp[xe], axis=-1, keepdims=True)
                lp[xe] = jnp.zeros((2 * ATT_Q, LANES), F32)
            if 0 <= xm < ATT_STREAMS:
                rho[xm] = finish(step, xm, None, lp[xm])
                acc[xm] = jnp.zeros((ATT_Q, DA_V_DIM), F32)
            for j in range(nk):
                if xs < ATT_STREAMS:
                    mp[xs] = scores(xs, qz[xs], j, mp[xs])
                if 0 <= xe < ATT_STREAMS:
                    lp[xe] = exps(xe, m[xe], j, lp[xe])
                if 0 <= xm < ATT_STREAMS:
                    acc[xm] = mix(xm, rho[xm], j, acc[xm])
            if 0 <= xm < ATT_STREAMS:
                finish(step, xm, acc[xm], lp[xm])
        return carry

    steps = seq // (ATT_STREAMS * ATT_Q)

    @pl.when(safe)
    def _():
        lax.fori_loop(0, steps, fast_step, 0)

    @pl.when(jnp.logical_not(safe))
    def _():
        lax.fori_loop(0, steps, slow_step, 0)


def _attention(lamv, subln, qd, kd, vd, batch, seq):
    n = qd.shape[0]
    assert seq % (ATT_STREAMS * ATT_Q) == 0
    slab = pl.BlockSpec((seq, DA_V_DIM), lambda b, h: (b, h))
    return pl.pallas_call(
        _attn_kernel,
        grid=(batch, DA_HEADS),
        in_specs=[pl.BlockSpec((4, DA_HEAD_DIM), lambda b, h: (0, 0)),
                  pl.BlockSpec((1, DA_V_DIM), lambda b, h: (0, 0)),
                  slab, slab, slab],
        out_specs=slab,
        out_shape=jax.ShapeDtypeStruct((n, DA_WIDTH), BF16),
        scratch_shapes=[pltpu.VMEM((2, 2 * ATT_Q, seq), F32),
                        pltpu.VMEM((2, 2 * ATT_Q, seq), BF16)],
        compiler_params=pltpu.CompilerParams(
            dimension_semantics=("parallel", "parallel"), vmem_limit_bytes=VMEM_LIMIT),
        name="diffattn",
    )(lamv, subln, qd, kd, vd)


def _cumsum_rows(x, rev):
    c = x.shape[0]
    row = lax.broadcasted_iota(jnp.int32, x.shape, 0)
    sh = 1
    while sh < c:
        if rev:
            x = x + jnp.where(row < c - sh, pltpu.roll(x, c - sh, 0), 0.0)
        else:
            x = x + jnp.where(row >= sh, pltpu.roll(x, sh, 0), 0.0)
        sh *= 2
    return x


def _block_row(g, half, rev):
    c, w = g.shape
    idx = half if rev else half - 1
    blk = 2 * half
    if blk >= SUBLANES:
        g3 = g.reshape(c // blk, blk, w)
        return jnp.broadcast_to(g3[:, idx:idx + 1, :], g3.shape).reshape(c, w)
    g3 = g.reshape(c // SUBLANES, SUBLANES, w)
    sub = lax.broadcasted_iota(jnp.int32, g3.shape, 1)
    out = None
    for b in reversed(range(SUBLANES // blk)):
        piece = jnp.broadcast_to(g3[:, b * blk + idx:b * blk + idx + 1, :], g3.shape)
        out = piece if out is None else jnp.where(sub < (b + 1) * blk, piece, out)
    return out.reshape(c, w)


def _gla_chunk(q, f, v, st_ref, pair_level, rev):
    c = q.shape[0]
    qf = q.astype(F32)
    vf = v.astype(F32)
    kk = 1.0 - f
    g = _cumsum_rows(jnp.log2(f), rev)
    row = lax.broadcasted_iota(jnp.int32, (c, 1), 0)

    st = st_ref[...]
    o = lax.dot_general((qf * jnp.exp2(g)).astype(BF16), st.astype(BF16), _NT,
                        preferred_element_type=F32)
    o = o + jnp.sum(qf * kk, axis=-1, keepdims=True) * vf

    a = jnp.zeros((c, c), F32)
    half = 1
    lvl = 0
    while half < c:
        second = (row & half) != 0
        tgt = jnp.logical_not(second) if rev else second
        if half == 1:
            d = jnp.where(tgt, f, 1.0)
        else:
            d = jnp.exp2((g - _block_row(g, half, rev)) * jnp.where(tgt, 1.0, -1.0))
        xf = jnp.where(tgt, qf, kk) * d
        x = xf.astype(BF16)
        if half >= SUBLANES:
            first = 0 if rev else half
            pick = lambda m: jnp.concatenate(
                [m[b + first:b + first + half] for b in range(0, c, 2 * half)], axis=0)
            al = lax.dot_general(pick(xf).astype(BF16), x, _NT, preferred_element_type=F32)
            upd = jnp.where(pick(pair_level) == lvl, al, pick(a))
            rows = []
            for i, b in enumerate(range(0, c, 2 * half)):
                other = a[b + half - first:b + 2 * half - first]
                mine = upd[i * half:(i + 1) * half]
                rows += [mine, other] if rev else [other, mine]
            a = jnp.concatenate(rows, axis=0)
        else:
            al = lax.dot_general(x, x, _NT, preferred_element_type=F32)
            a = jnp.where(pair_level == lvl, al, a)
        half *= 2
        lvl += 1
    o = o + jnp.dot(a.astype(BF16), v, preferred_element_type=F32)

    g_tot = g[0:1, :] if rev else g[c - 1:c, :]
    ks = (kk * jnp.exp2(g_tot - g)).astype(BF16)
    st_ref[...] = st * jnp.exp2(g_tot) + lax.dot_general(v, ks, _TN, preferred_element_type=F32)
    return o


def _hgrn_kernel(lv_ref, qf_ref, ff_ref, vf_ref, qb_ref, fb_ref, vb_ref, of_ref, ob_ref, st_ref):
    @pl.when(pl.program_id(1) == 0)
    def _():
        st_ref[...] = jnp.zeros_like(st_ref)

    lv_fwd = lv_ref[0]
    lv_bwd = lv_ref[1]
    for r in range(HG_ROWS):
        for h in range(HG_HEADS):
            sl = slice(h * HG_HEAD_DIM, (h + 1) * HG_HEAD_DIM)
            of_ref[r, :, sl] = _gla_chunk(qf_ref[r, :, sl], ff_ref[r, :, sl], vf_ref[r, :, sl],
                                          st_ref.at[r, h], lv_fwd, False)
            ob_ref[r, :, sl] = _gla_chunk(qb_ref[r, :, sl], fb_ref[r, :, sl], vb_ref[r, :, sl],
                                          st_ref.at[r, HG_HEADS + h], lv_bwd, True)


def _pair_levels(c):
    t = jnp.arange(c, dtype=jnp.int32)[:, None]
    s = jnp.arange(c, dtype=jnp.int32)[None, :]
    x = t ^ s
    lv = jnp.full((c, c), -1, jnp.int32)
    for i in range(c.bit_length() - 1):
        lv = jnp.where((x >> i) == 1, i, lv)
    return jnp.stack([jnp.where(t > s, lv, -1), jnp.where(t < s, lv, -1)])


def _hgrn(qh, ff, fb, ih, batch, seq):
    c = HG_CHUNK
    assert batch % HG_ROWS == 0
    nc = seq // c
    fwd = pl.BlockSpec((HG_ROWS, c, HG_WIDTH), lambda b, j: (b, j, 0))
    bwd = pl.BlockSpec((HG_ROWS, c, HG_WIDTH), lambda b, j: (b, nc - 1 - j, 0))
    qh, ff, fb, ih = (a.reshape(batch, seq, HG_WIDTH) for a in (qh, ff, fb, ih))
    return pl.pallas_call(
        _hgrn_kernel,
        grid=(batch // HG_ROWS, nc),
        in_specs=[pl.BlockSpec((2, c, c), lambda b, j: (0, 0, 0)), fwd, fwd, fwd, bwd, bwd, bwd],
        out_specs=[fwd, bwd],
        out_shape=[jax.ShapeDtypeStruct((batch, seq, HG_WIDTH), F32)] * 2,
        scratch_shapes=[pltpu.VMEM((HG_ROWS, 2 * HG_HEADS, HG_HEAD_DIM, HG_HEAD_DIM), F32)],
        compiler_params=pltpu.CompilerParams(dimension_semantics=("parallel", "arbitrary"),
                                             vmem_limit_bytes=VMEM_LIMIT),
        name="hgrn2",
    )(_pair_levels(c), qh, ff, ih, qh, fb, ih)


def _merge_mlp_kernel(x_ref, of_ref, ob_ref, gh_ref, oa_ref, sgh_ref, sgd_ref,
                      hgn_ref, n2_ref, fn_ref, whg_ref, wda_ref, wout_ref, w1_ref, w2_ref, y_ref):
    o = of_ref[...] + ob_ref[...]
    parts = []
    for h in range(HG_HEADS):
        sl = slice(h * HG_HEAD_DIM, (h + 1) * HG_HEAD_DIM)
        oh = o[:, sl]
        ms = jnp.mean(oh * oh, axis=-1, keepdims=True)
        parts.append(oh * lax.rsqrt(ms + NORM_EPS) * hgn_ref[:, sl])
    o = jnp.concatenate(parts, axis=1) * gh_ref[...].astype(F32)
    y_hg = jnp.dot(o.astype(BF16), whg_ref[...], preferred_element_type=F32)
    y_da = jnp.dot(oa_ref[...], wda_ref[...], preferred_element_type=F32)
    m = sgh_ref[...].astype(F32) * y_hg + sgd_ref[...].astype(F32) * y_da
    x1 = x_ref[...] + jnp.dot(m.astype(BF16), wout_ref[...], preferred_element_type=F32)

    ms = jnp.mean(x1 * x1, axis=-1, keepdims=True)
    h2 = (x1 * lax.rsqrt(ms + NORM_EPS) * n2_ref[...]).astype(BF16)
    acc = x1
    for j in range(D_FF // MLP_COLS):
        cs = slice(j * MLP_COLS, (j + 1) * MLP_COLS)
        mid = jnp.maximum(jnp.dot(h2, w1_ref[:, cs], preferred_element_type=F32), 0.0)
        acc = acc + jnp.dot((mid * mid).astype(BF16), w2_ref[cs, :], preferred_element_type=F32)
    ms = jnp.mean(acc * acc, axis=-1, keepdims=True)
    y_ref[...] = acc * lax.rsqrt(ms + NORM_EPS) * fn_ref[...]


def _merge_mlp(x2, of, ob, gh, oa, sgh, sgd, hgn, n2, fn, whg, wda, wout, w1, w2):
    n = x2.shape[0]
    tm = MLP_ROWS
    row = lambda w: pl.BlockSpec((tm, w), lambda i: (i, 0))
    return pl.pallas_call(
        _merge_mlp_kernel,
        grid=(n // tm,),
        in_specs=[row(D_MODEL), row(HG_WIDTH), row(HG_WIDTH), row(HG_WIDTH), row(DA_WIDTH),
                  row(D_MODEL), row(D_MODEL),
                  _const_spec((1, HG_WIDTH)), _const_spec((1, D_MODEL)), _const_spec((1, D_MODEL)),
                  _const_spec((HG_WIDTH, D_MODEL)), _const_spec((DA_WIDTH, D_MODEL)),
                  _const_spec((D_MODEL, D_MODEL)), _const_spec((D_MODEL, D_FF)),
                  _const_spec((D_FF, D_MODEL))],
        out_specs=row(D_MODEL),
        out_shape=jax.ShapeDtypeStruct((n, D_MODEL), F32),
        compiler_params=pltpu.CompilerParams(dimension_semantics=("parallel",),
                                             vmem_limit_bytes=VMEM_LIMIT),
        name="merge_mlp",
    )(x2, of, ob, gh, oa, sgh, sgd, hgn, n2, fn, whg, wda, wout, w1, w2)


def _rotary_table(seq):
    pos = jnp.arange(seq, dtype=F32)
    inv_freq = ROPE_THETA ** (-jnp.arange(0, ROT_DIM, 2, dtype=F32) / ROT_DIM)
    ang = pos[:, None] * inv_freq[None, :]
    cs = jnp.concatenate([jnp.cos(ang), jnp.sin(ang)], axis=1)
    lane = jnp.arange(3 * LANES)
    table, c = lane // LANES, lane % DA_HEAD_DIM
    src_row = jnp.arange(2 * ROT_HALF)[:, None]
    is_cos = (table == 0) & (c < ROT_DIM) & (src_row == c % ROT_HALF)
    is_lo = (table == 1) & (c >= ROT_HALF) & (c < ROT_DIM) & (src_row == ROT_HALF + c % ROT_HALF)
    is_hi = (table == 2) & (c < ROT_HALF) & (src_row == ROT_HALF + c)
    sel = jnp.where(is_cos | is_lo, 1.0, 0.0) - jnp.where(is_hi, 1.0, 0.0)
    ones = jnp.where((table == 0) & (c >= ROT_DIM), 1.0, 0.0)[None, :]
    return jnp.dot(cs, sel.astype(F32), precision=lax.Precision.HIGHEST) + ones


def _trunk(x, p, rot):
    batch, seq, _ = x.shape
    assert seq % max(IN_ROWS, ATT_STREAMS * ATT_Q, ATT_K, HG_CHUNK, MLP_ROWS) == 0
    x2 = x.reshape(batch * seq, D_MODEL)
    qd, kd, vd, qh, ff, fb, ih, gh, sgh, sgd = _inproj(x2, p["n1"], p["w_in"], p["lbl"], rot, seq)
    oa = _attention(p["lamv"], p["subln"], qd, kd, vd, batch, seq)
    of, ob = (a.reshape(batch * seq, HG_WIDTH) for a in _hgrn(qh, ff, fb, ih, batch, seq))
    y = _merge_mlp(x2, of, ob, gh, oa, sgh, sgd, p["hgn"], p["n2"], p["fn"],
                   p["whg"], p["wda"], p["wout"], p["w1"], p["w2"])
    return y.reshape(batch, seq, D_MODEL)


def kernel(x_prompt, x_sample, norm1, w_in, hg_lb_logits, hg_norm, w_hg_branch, da_lambda_q1, da_lambda_k1, da_lambda_q2, da_lambda_k2, da_subln, w_da_branch, w_out, norm2, w_mlp_in, w_mlp_out, final_norm):
    assert norm1.shape[0] == 1 and hg_lb_logits.shape == (2, 2, HG_WIDTH)
    p = dict(
        n1=norm1.reshape(1, D_MODEL), w_in=w_in[0].astype(BF16),
        lbl=hg_lb_logits.reshape(4, HG_WIDTH),
        lamv=jnp.concatenate([da_lambda_q1, da_lambda_k1, da_lambda_q2, da_lambda_k2], axis=0),
        subln=da_subln.reshape(1, DA_V_DIM), hgn=hg_norm.reshape(1, HG_WIDTH),
        n2=norm2.reshape(1, D_MODEL), fn=final_norm.reshape(1, D_MODEL),
        whg=w_hg_branch[0].astype(BF16), wda=w_da_branch[0].astype(BF16), wout=w_out[0].astype(BF16),
        w1=w_mlp_in[0].astype(BF16), w2=w_mlp_out[0].astype(BF16),
    )
    rot = _rotary_table(max(x_prompt.shape[1], x_sample.shape[1]))
    return _trunk(x_prompt, p, rot), _trunk(x_sample, p, rot)
```

```python
import math

import jax
import jax.numpy as jnp
from jax import lax
from jax.experimental import pallas as pl
from jax.experimental.pallas import tpu as pltpu

F32 = jnp.float32
BF16 = jnp.bfloat16

D_MODEL = 1024
HG_HEADS = 4
HG_HEAD_DIM = 128
HG_WIDTH = HG_HEADS * HG_HEAD_DIM
HG_SCALE = HG_HEAD_DIM ** -0.5
DA_HEADS = 4
DA_HEAD_DIM = 64
DA_V_DIM = 2 * DA_HEAD_DIM
DA_WIDTH = DA_HEADS * DA_V_DIM
DA_SCALE = DA_HEAD_DIM ** -0.5
ROT_DIM = DA_HEAD_DIM // 4
ROT_HALF = ROT_DIM // 2
ROPE_THETA = 500000.0
D_FF = 4 * D_MODEL
NORM_EPS = 1e-6
SUBLN_EPS = 1e-5
IN_WIDTH = 3 * DA_WIDTH + 5 * HG_WIDTH + 2 * D_MODEL
LAM_INIT = 0.8 - 0.6 * math.exp(-0.3 * 0)

LANES = 128
SUBLANES = 8
VMEM_LIMIT = 56 * 1024 * 1024

SEG = 512
IN_ROWS = 1024
ATT_Q = 128
ATT_STREAMS = 8
ATT_K = 512
LOG2_E = math.log2(math.e)
ATT_SAFE_LOG2 = 48.0
ATT_SAFE_LAM = 1024.0
HG_CHUNK = 128
HG_ROWS = 4
MLP_ROWS = 512
MLP_COLS = 1024

_NT = (((1,), (1,)), ((), ()))
_TN = (((0,), (0,)), ((), ()))


def _sigmoid(x):
    return 1.0 / (1.0 + jnp.exp(-x))


def _const_spec(shape):
    nd = len(shape)
    return pl.BlockSpec(shape, lambda *_: (0,) * nd, pipeline_mode=pl.Buffered(1))


def _inproj_kernel(x_ref, n1_ref, w_ref, lbl_ref, rot_ref,
                   qd_ref, kd_ref, vd_ref, qh_ref, ff_ref, fb_ref, ih_ref, gh_ref, sgh_ref, sgd_ref):
    x = x_ref[...]
    ms = jnp.mean(x * x, axis=-1, keepdims=True)
    h = (x * lax.rsqrt(ms + NORM_EPS) * n1_ref[...]).astype(BF16)

    def seg(j):
        return jnp.dot(h, w_ref[:, j * SEG:(j + 1) * SEG], preferred_element_type=F32)

    cos_t = rot_ref[:, 0:LANES]
    sin_lo = rot_ref[:, LANES:2 * LANES]
    sin_hi = rot_ref[:, 2 * LANES:3 * LANES]

    def rope_store(u, out_ref, scale):
        for p in range(SEG // LANES):
            a = u[:, p * LANES:(p + 1) * LANES]
            r = (a * cos_t + pltpu.roll(a, ROT_HALF, 1) * sin_lo
                 + pltpu.roll(a, LANES - ROT_HALF, 1) * sin_hi)
            if scale != 1.0:
                r = r * scale
            out_ref[:, p * LANES:(p + 1) * LANES] = r.astype(out_ref.dtype)

    rope_store(seg(0), qd_ref, DA_SCALE * LOG2_E)
    rope_store(seg(1), kd_ref, 1.0)
    vd_ref[...] = seg(2).astype(BF16)

    u = seg(3)
    qh_ref[...] = (u * _sigmoid(u) * HG_SCALE).astype(BF16)

    def lower_bound(d):
        l0 = lbl_ref[2 * d:2 * d + 1, :]
        l1 = lbl_ref[2 * d + 1:2 * d + 2, :]
        m = jnp.maximum(l0, l1)
        e0 = jnp.exp(l0 - m)
        e1 = jnp.exp(l1 - m)
        return e0 / (e0 + e1)

    lb = lower_bound(0)
    ff_ref[...] = lb + (1.0 - lb) * _sigmoid(seg(4))
    lb = lower_bound(1)
    fb_ref[...] = lb + (1.0 - lb) * _sigmoid(seg(5))
    ih_ref[...] = seg(6).astype(BF16)
    u = seg(7)
    gh_ref[...] = (u * _sigmoid(u)).astype(BF16)
    for p in range(D_MODEL // SEG):
        sgh_ref[:, p * SEG:(p + 1) * SEG] = _sigmoid(seg(8 + p)).astype(BF16)
        sgd_ref[:, p * SEG:(p + 1) * SEG] = _sigmoid(seg(10 + p)).astype(BF16)


def _inproj(x2, n1, w_in, lbl, rot, seq):
    n = x2.shape[0]
    tm = min(IN_ROWS, seq)
    nt = seq // tm
    row = lambda w: pl.BlockSpec((tm, w), lambda i: (i, 0))
    out_w = (DA_WIDTH,) * 3 + (HG_WIDTH,) * 5 + (D_MODEL,) * 2
    out_dt = (BF16, BF16, BF16, BF16, F32, F32, BF16, BF16, BF16, BF16)
    return pl.pallas_call(
        _inproj_kernel,
        grid=(n // tm,),
        in_specs=[row(D_MODEL), _const_spec((1, D_MODEL)), _const_spec((D_MODEL, IN_WIDTH)),
                  _const_spec((4, HG_WIDTH)),
                  pl.BlockSpec((tm, 3 * LANES), lambda i: (i % nt, 0))],
        out_specs=[row(w) for w in out_w],
        out_shape=[jax.ShapeDtypeStruct((n, w), dt) for w, dt in zip(out_w, out_dt)],
        compiler_params=pltpu.CompilerParams(dimension_semantics=("parallel",),
                                             vmem_limit_bytes=VMEM_LIMIT),
        name="inproj",
    )(x2, n1, w_in, lbl, rot)


def _attn_kernel(lam_ref, sub_ref, q_ref, k_ref, v_ref, o_ref, s_ref, p_ref):
    seq = k_ref.shape[0]
    nk = seq // ATT_K
    tiles = ATT_K // LANES

    def bound_check():
        d = lax.broadcasted_iota(jnp.int32, (DA_V_DIM, LANES), 0)
        c = lax.broadcasted_iota(jnp.int32, (DA_V_DIM, LANES), 1)
        pick = jnp.where(c == d // DA_HEAD_DIM, 1.0, 0.0).astype(BF16)

        def max_norm2(ref):
            a = ref[...].astype(F32)
            return jnp.max(jnp.dot((a * a).astype(BF16), pick, preferred_element_type=F32), axis=0, keepdims=True)

        b2 = max_norm2(q_ref) * max_norm2(k_ref)
        b2 = jnp.maximum(b2[:, 0:1], b2[:, 1:2])
        return b2[0, 0] <= 0.98 * ATT_SAFE_LOG2 * ATT_SAFE_LOG2

    lv = lam_ref[...]
    lam = (jnp.exp(jnp.sum(lv[0:1] * lv[1:2], axis=-1, keepdims=True))
           - jnp.exp(jnp.sum(lv[2:3] * lv[3:4], axis=-1, keepdims=True)) + LAM_INIT)
    safe = jnp.logical_and(bound_check(), jnp.abs(lam[0, 0]) <= ATT_SAFE_LAM)
    lane = lax.broadcasted_iota(jnp.int32, (ATT_Q, DA_V_DIM), 1)

    def block_rows(step, x):
        return pl.ds(pl.multiple_of((step * ATT_STREAMS + x) * ATT_Q, ATT_Q), ATT_Q)

    def masked_q(step, x):
        q = q_ref[block_rows(step, x), :]
        zero = jnp.zeros_like(q)
        return jnp.concatenate([jnp.where(lane < DA_HEAD_DIM, q, zero),
                                jnp.where(lane >= DA_HEAD_DIM, q, zero)], axis=0)

    def scores(x, qz, j, mp):
        ks = slice(j * ATT_K, (j + 1) * ATT_K)
        s = lax.dot_general(qz, k_ref[ks, :], _NT, preferred_element_type=F32)
        s_ref[x % 2, :, ks] = s
        for t in range(tiles):
            mp = jnp.maximum(mp, s[:, t * LANES:(t + 1) * LANES])
        return mp

    def exps(x, m, j, lp):
        ks = slice(j * ATT_K, (j + 1) * ATT_K)
        p = jnp.exp2(s_ref[x % 2, :, ks] - m)
        for t in range(tiles):
            lp = lp + p[:, t * LANES:(t + 1) * LANES]
        p_ref[x % 2, :, ks] = p.astype(BF16)
        return lp

    def fused(x, qz, j, lp):
        ks = slice(j * ATT_K, (j + 1) * ATT_K)
        p = jnp.exp2(lax.dot_general(qz, k_ref[ks, :], _NT, preferred_element_type=F32))
        for t in range(tiles):
            lp = lp + p[:, t * LANES:(t + 1) * LANES]
        p_ref[x % 2, :, ks] = p.astype(BF16)
        return lp

    def mix(x, rho, j, acc):
        ks = slice(j * ATT_K, (j + 1) * ATT_K)
        w = p_ref[x % 2, :ATT_Q, ks] - rho * p_ref[x % 2, ATT_Q:, ks]
        return acc + jnp.dot(w, v_ref[ks, :], preferred_element_type=F32)

    def finish(step, x, acc, lp):
        l = jnp.sum(lp, axis=-1, keepdims=True)
        if acc is None:
            return (lam * l[:ATT_Q] / l[ATT_Q:]).astype(BF16)
        o = acc * (1.0 / l[:ATT_Q])
        ms = jnp.mean(o * o, axis=-1, keepdims=True)
        o = o * lax.rsqrt(ms + SUBLN_EPS) * sub_ref[...] * (1.0 - LAM_INIT)
        o_ref[block_rows(step, x), :] = o.astype(o_ref.dtype)

    def fast_step(step, carry):
        lps = [None] * ATT_STREAMS
        for t in range(ATT_STREAMS + 1):
            if t < ATT_STREAMS:
                qz = masked_q(step, t)
                lp = jnp.zeros((2 * ATT_Q, LANES), F32)
                for j in range(nk):
                    lp = fused(t, qz, j, lp)
                lps[t] = lp
            if t >= 1:
                x = t - 1
                rho = finish(step, x, None, lps[x])
                acc = jnp.zeros((ATT_Q, DA_V_DIM), F32)
                for j in range(nk):
                    acc = mix(x, rho, j, acc)
                finish(step, x, acc, lps[x])
        return carry

    def slow_step(step, carry):
        qz = [None] * ATT_STREAMS
        mp = [None] * ATT_STREAMS
        lp = [None] * ATT_STREAMS
        m = [None] * ATT_STREAMS
        rho = [None] * ATT_STREAMS
        acc = [None] * ATT_STREAMS
        for t in range(ATT_STREAMS + 2):
            xs, xe, xm = t, t - 1, t - 2
            if xs < ATT_STREAMS:
                qz[xs] = masked_q(step, xs)
                mp[xs] = jnp.full((2 * ATT_Q, LANES), -jnp.inf, F32)
            if 0 <= xe < ATT_STREAMS:
                m[xe] = jnp.max(mp[xe], axis=-1, keepdims=True)
                lp[xe] = jnp.zeros((2 * ATT_Q, LANES), F32)
            if 0 <= xm < ATT_STREAMS:
                rho[xm] = finish(step, xm, None, lp[xm])
                acc[xm] = jnp.zeros((ATT_Q, DA_V_DIM), F32)
            for j in range(nk):
                if xs < ATT_STREAMS:
                    mp[xs] = scores(xs, qz[xs], j, mp[xs])
                if 0 <= xe < ATT_STREAMS:
                    lp[xe] = exps(xe, m[xe], j, lp[xe])
                if 0 <= xm < ATT_STREAMS:
                    acc[xm] = mix(xm, rho[xm], j, acc[xm])
            if 0 <= xm < ATT_STREAMS:
                finish(step, xm, acc[xm], lp[xm])
        return carry

    steps = seq // (ATT_STREAMS * ATT_Q)

    @pl.when(safe)
    def _():
        lax.fori_loop(0, steps, fast_step, 0)

    @pl.when(jnp.logical_not(safe))
    def _():
        lax.fori_loop(0, steps, slow_step, 0)


def _attention(lamv, subln, qd, kd, vd, batch, seq):
    n = qd.shape[0]
    assert seq % (ATT_STREAMS * ATT_Q) == 0
    slab = pl.BlockSpec((seq, DA_V_DIM), lambda b, h: (b, h))
    return pl.pallas_call(
        _attn_kernel,
        grid=(batch, DA_HEADS),
        in_specs=[pl.BlockSpec((4, DA_HEAD_DIM), lambda b, h: (0, 0)),
                  pl.BlockSpec((1, DA_V_DIM), lambda b, h: (0, 0)),
                  slab, slab, slab],
        out_specs=slab,
        out_shape=jax.ShapeDtypeStruct((n, DA_WIDTH), BF16),
        scratch_shapes=[pltpu.VMEM((2, 2 * ATT_Q, seq), F32),
                        pltpu.VMEM((2, 2 * ATT_Q, seq), BF16)],
        compiler_params=pltpu.CompilerParams(
            dimension_semantics=("parallel", "parallel"), vmem_limit_bytes=VMEM_LIMIT),
        name="diffattn",
    )(lamv, subln, qd, kd, vd)


def _cumsum_rows(x, rev):
    c = x.shape[0]
    row = lax.broadcasted_iota(jnp.int32, x.shape, 0)
    sh = 1
    while sh < c:
        if rev:
            x = x + jnp.where(row < c - sh, pltpu.roll(x, c - sh, 0), 0.0)
        else:
            x = x + jnp.where(row >= sh, pltpu.roll(x, sh, 0), 0.0)
        sh *= 2
    return x


def _block_row(g, half, rev):
    c, w = g.shape
    idx = half if rev else half - 1
    blk = 2 * half
    if blk >= SUBLANES:
        g3 = g.reshape(c // blk, blk, w)
        return jnp.broadcast_to(g3[:, idx:idx + 1, :], g3.shape).reshape(c, w)
    g3 = g.reshape(c // SUBLANES, SUBLANES, w)
    sub = lax.broadcasted_iota(jnp.int32, g3.shape, 1)
    out = None
    for b in reversed(range(SUBLANES // blk)):
        piece = jnp.broadcast_to(g3[:, b * blk + idx:b * blk + idx + 1, :], g3.shape)
        out = piece if out is None else jnp.where(sub < (b + 1) * blk, piece, out)
    return out.reshape(c, w)


def _gla_chunk(q, f, v, st_ref, pair_level, rev):
    c = q.shape[0]
    qf = q.astype(F32)
    vf = v.astype(F32)
    kk = 1.0 - f
    g = _cumsum_rows(jnp.log2(f), rev)
    row = lax.broadcasted_iota(jnp.int32, (c, 1), 0)

    st = st_ref[...]
    o = lax.dot_general((qf * jnp.exp2(g)).astype(BF16), st.astype(BF16), _NT,
                        preferred_element_type=F32)
    o = o + jnp.sum(qf * kk, axis=-1, keepdims=True) * vf

    a = jnp.zeros((c, c), F32)
    half = 1
    lvl = 0
    while half < c:
        second = (row & half) != 0
        tgt = jnp.logical_not(second) if rev else second
        if half == 1:
            d = jnp.where(tgt, f, 1.0)
        else:
            d = jnp.exp2((g - _block_row(g, half, rev)) * jnp.where(tgt, 1.0, -1.0))
        xf = jnp.where(tgt, qf, kk) * d
        x = xf.astype(BF16)
        if half >= SUBLANES:
            first = 0 if rev else half
            pick = lambda m: jnp.concatenate(
                [m[b + first:b + first + half] for b in range(0, c, 2 * half)], axis=0)
            al = lax.dot_general(pick(xf).astype(BF16), x, _NT, preferred_element_type=F32)
            upd = jnp.where(pick(pair_level) == lvl, al, pick(a))
            rows = []
            for i, b in enumerate(range(0, c, 2 * half)):
                other = a[b + half - first:b + 2 * half - first]
                mine = upd[i * half:(i + 1) * half]
                rows += [mine, other] if rev else [other, mine]
            a = jnp.concatenate(rows, axis=0)
        else:
            al = lax.dot_general(x, x, _NT, preferred_element_type=F32)
            a = jnp.where(pair_level == lvl, al, a)
        half *= 2
        lvl += 1
    o = o + jnp.dot(a.astype(BF16), v, preferred_element_type=F32)

    g_tot = g[0:1, :] if rev else g[c - 1:c, :]
    ks = (kk * jnp.exp2(g_tot - g)).astype(BF16)
    st_ref[...] = st * jnp.exp2(g_tot) + lax.dot_general(v, ks, _TN, preferred_element_type=F32)
    return o


def _hgrn_kernel(lv_ref, qf_ref, ff_ref, vf_ref, qb_ref, fb_ref, vb_ref, of_ref, ob_ref, st_ref):
    @pl.when(pl.program_id(1) == 0)
    def _():
        st_ref[...] = jnp.zeros_like(st_ref)

    lv_fwd = lv_ref[0]
    lv_bwd = lv_ref[1]
    for r in range(HG_ROWS):
        for h in range(HG_HEADS):
            sl = slice(h * HG_HEAD_DIM, (h + 1) * HG_HEAD_DIM)
            of_ref[r, :, sl] = _gla_chunk(qf_ref[r, :, sl], ff_ref[r, :, sl], vf_ref[r, :, sl],
                                          st_ref.at[r, h], lv_fwd, False)
            ob_ref[r, :, sl] = _gla_chunk(qb_ref[r, :, sl], fb_ref[r, :, sl], vb_ref[r, :, sl],
                                          st_ref.at[r, HG_HEADS + h], lv_bwd, True)


def _pair_levels(c):
    t = jnp.arange(c, dtype=jnp.int32)[:, None]
    s = jnp.arange(c, dtype=jnp.int32)[None, :]
    x = t ^ s
    lv = jnp.full((c, c), -1, jnp.int32)
    for i in range(c.bit_length() - 1):
        lv = jnp.where((x >> i) == 1, i, lv)
    return jnp.stack([jnp.where(t > s, lv, -1), jnp.where(t < s, lv, -1)])


def _hgrn(qh, ff, fb, ih, batch, seq):
    c = HG_CHUNK
    assert batch % HG_ROWS == 0
    nc = seq // c
    fwd = pl.BlockSpec((HG_ROWS, c, HG_WIDTH), lambda b, j: (b, j, 0))
    bwd = pl.BlockSpec((HG_ROWS, c, HG_WIDTH), lambda b, j: (b, nc - 1 - j, 0))
    qh, ff, fb, ih = (a.reshape(batch, seq, HG_WIDTH) for a in (qh, ff, fb, ih))
    return pl.pallas_call(
        _hgrn_kernel,
        grid=(batch // HG_ROWS, nc),
        in_specs=[pl.BlockSpec((2, c, c), lambda b, j: (0, 0, 0)), fwd, fwd, fwd, bwd, bwd, bwd],
        out_specs=[fwd, bwd],
        out_shape=[jax.ShapeDtypeStruct((batch, seq, HG_WIDTH), F32)] * 2,
        scratch_shapes=[pltpu.VMEM((HG_ROWS, 2 * HG_HEADS, HG_HEAD_DIM, HG_HEAD_DIM), F32)],
        compiler_params=pltpu.CompilerParams(dimension_semantics=("parallel", "arbitrary"),
                                             vmem_limit_bytes=VMEM_LIMIT),
        name="hgrn2",
    )(_pair_levels(c), qh, ff, ih, qh, fb, ih)


def _merge_mlp_kernel(x_ref, of_ref, ob_ref, gh_ref, oa_ref, sgh_ref, sgd_ref,
                      hgn_ref, n2_ref, fn_ref, whg_ref, wda_ref, wout_ref, w1_ref, w2_ref, y_ref):
    o = of_ref[...] + ob_ref[...]
    parts = []
    for h in range(HG_HEADS):
        sl = slice(h * HG_HEAD_DIM, (h + 1) * HG_HEAD_DIM)
        oh = o[:, sl]
        ms = jnp.mean(oh * oh, axis=-1, keepdims=True)
        parts.append(oh * lax.rsqrt(ms + NORM_EPS) * hgn_ref[:, sl])
    o = jnp.concatenate(parts, axis=1) * gh_ref[...].astype(F32)
    y_hg = jnp.dot(o.astype(BF16), whg_ref[...], preferred_element_type=F32)
    y_da = jnp.dot(oa_ref[...], wda_ref[...], preferred_element_type=F32)
    m = sgh_ref[...].astype(F32) * y_hg + sgd_ref[...].astype(F32) * y_da
    x1 = x_ref[...] + jnp.dot(m.astype(BF16), wout_ref[...], preferred_element_type=F32)

    ms = jnp.mean(x1 * x1, axis=-1, keepdims=True)
    h2 = (x1 * lax.rsqrt(ms + NORM_EPS) * n2_ref[...]).astype(BF16)
    acc = x1
    for j in range(D_FF // MLP_COLS):
        cs = slice(j * MLP_COLS, (j + 1) * MLP_COLS)
        mid = jnp.maximum(jnp.dot(h2, w1_ref[:, cs], preferred_element_type=F32), 0.0)
        acc = acc + jnp.dot((mid * mid).astype(BF16), w2_ref[cs, :], preferred_element_type=F32)
    ms = jnp.mean(acc * acc, axis=-1, keepdims=True)
    y_ref[...] = acc * lax.rsqrt(ms + NORM_EPS) * fn_ref[...]


def _merge_mlp(x2, of, ob, gh, oa, sgh, sgd, hgn, n2, fn, whg, wda, wout, w1, w2):
    n = x2.shape[0]
    tm = MLP_ROWS
    row = lambda w: pl.BlockSpec((tm, w), lambda i: (i, 0))
    return pl.pallas_call(
        _merge_mlp_kernel,
        grid=(n // tm,),
        in_specs=[row(D_MODEL), row(HG_WIDTH), row(HG_WIDTH), row(HG_WIDTH), row(DA_WIDTH),
                  row(D_MODEL), row(D_MODEL),
                  _const_spec((1, HG_WIDTH)), _const_spec((1, D_MODEL)), _const_spec((1, D_MODEL)),
                  _const_spec((HG_WIDTH, D_MODEL)), _const_spec((DA_WIDTH, D_MODEL)),
                  _const_spec((D_MODEL, D_MODEL)), _const_spec((D_MODEL, D_FF)),
                  _const_spec((D_FF, D_MODEL))],
        out_specs=row(D_MODEL),
        out_shape=jax.ShapeDtypeStruct((n, D_MODEL), F32),
        compiler_params=pltpu.CompilerParams(dimension_semantics=("parallel",),
                                             vmem_limit_bytes=VMEM_LIMIT),
        name="merge_mlp",
    )(x2, of, ob, gh, oa, sgh, sgd, hgn, n2, fn, whg, wda, wout, w1, w2)


def _rotary_table(seq):
    pos = jnp.arange(seq, dtype=F32)
    inv_freq = ROPE_THETA ** (-jnp.arange(0, ROT_DIM, 2, dtype=F32) / ROT_DIM)
    ang = pos[:, None] * inv_freq[None, :]
    cs = jnp.concatenate([jnp.cos(ang), jnp.sin(ang)], axis=1)
    lane = jnp.arange(3 * LANES)
    table, c = lane // LANES, lane % DA_HEAD_DIM
    src_row = jnp.arange(2 * ROT_HALF)[:, None]
    is_cos = (table == 0) & (c < ROT_DIM) & (src_row == c % ROT_HALF)
    is_lo = (table == 1) & (c >= ROT_HALF) & (c < ROT_DIM) & (src_row == ROT_HALF + c % ROT_HALF)
    is_hi = (table == 2) & (c < ROT_HALF) & (src_row == ROT_HALF + c)
    sel = jnp.where(is_cos | is_lo, 1.0, 0.0) - jnp.where(is_hi, 1.0, 0.0)
    ones = jnp.where((table == 0) & (c >= ROT_DIM), 1.0, 0.0)[None, :]
    return jnp.dot(cs, sel.astype(F32), precision=lax.Precision.HIGHEST) + ones


def _trunk(x, p, rot):
    batch, seq, _ = x.shape
    assert seq % max(IN_ROWS, ATT_STREAMS * ATT_Q, ATT_K, HG_CHUNK, MLP_ROWS) == 0
    x2 = x.reshape(batch * seq, D_MODEL)
    qd, kd, vd, qh, ff, fb, ih, gh, sgh, sgd = _inproj(x2, p["n1"], p["w_in"], p["lbl"], rot, seq)
    oa = _attention(p["lamv"], p["subln"], qd, kd, vd, batch, seq)
    of, ob = (a.reshape(batch * seq, HG_WIDTH) for a in _hgrn(qh, ff, fb, ih, batch, seq))
    y = _merge_mlp(x2, of, ob, gh, oa, sgh, sgd, p["hgn"], p["n2"], p["fn"],
                   p["whg"], p["wda"], p["wout"], p["w1"], p["w2"])
    return y.reshape(batch, seq, D_MODEL)


def kernel(x_prompt, x_sample, norm1, w_in, hg_lb_logits, hg_norm, w_hg_branch, da_lambda_q1, da_lambda_k1, da_lambda_q2, da_lambda_k2, da_subln, w_da_branch, w_out, norm2, w_mlp_in, w_mlp_out, final_norm):
    assert norm1.shape[0] == 1 and hg_lb_logits.shape == (2, 2, HG_WIDTH)
    p = dict(
        n1=norm1.reshape(1, D_MODEL), w_in=w_in[0].astype(BF16),
        lbl=hg_lb_logits.reshape(4, HG_WIDTH),
        lamv=jnp.concatenate([da_lambda_q1, da_lambda_k1, da_lambda_q2, da_lambda_k2], axis=0),
        subln=da_subln.reshape(1, DA_V_DIM), hgn=hg_norm.reshape(1, HG_WIDTH),
        n2=norm2.reshape(1, D_MODEL), fn=final_norm.reshape(1, D_MODEL),
        whg=w_hg_branch[0].astype(BF16), wda=w_da_branch[0].astype(BF16), wout=w_out[0].astype(BF16),
        w1=w_mlp_in[0].astype(BF16), w2=w_mlp_out[0].astype(BF16),
    )
    rot = _rotary_table(max(x_prompt.shape[1], x_sample.shape[1]))
    return _trunk(x_prompt, p, rot), _trunk(x_sample, p, rot)
```

```python
import math

import jax
import jax.numpy as jnp
from jax import lax
from jax.experimental import pallas as pl
from jax.experimental.pallas import tpu as pltpu

F32 = jnp.float32
BF16 = jnp.bfloat16

D_MODEL = 1024
HG_HEADS = 4
HG_HEAD_DIM = 128
HG_WIDTH = HG_HEADS * HG_HEAD_DIM
HG_SCALE = HG_HEAD_DIM ** -0.5
DA_HEADS = 4
DA_HEAD_DIM = 64
DA_V_DIM = 2 * DA_HEAD_DIM
DA_WIDTH = DA_HEADS * DA_V_DIM
DA_SCALE = DA_HEAD_DIM ** -0.5
ROT_DIM = DA_HEAD_DIM // 4
ROT_HALF = ROT_DIM // 2
ROPE_THETA = 500000.0
D_FF = 4 * D_MODEL
NORM_EPS = 1e-6
SUBLN_EPS = 1e-5
IN_WIDTH = 3 * DA_WIDTH + 5 * HG_WIDTH + 2 * D_MODEL
LAM_INIT = 0.8 - 0.6 * math.exp(-0.3 * 0)

LANES = 128
SUBLANES = 8
VMEM_LIMIT = 56 * 1024 * 1024

SEG = 512
IN_ROWS = 1024
ATT_Q = 128
ATT_STREAMS = 16
ATT_K = 512
LOG2_E = math.log2(math.e)
ATT_SAFE_LOG2 = 48.0
ATT_SAFE_LAM = 1024.0
HG_CHUNK = 128
HG_ROWS = 8
MLP_ROWS = 512
MLP_COLS = 1024

_NT = (((1,), (1,)), ((), ()))
_TN = (((0,), (0,)), ((), ()))


def _sigmoid(x):
    return 1.0 / (1.0 + jnp.exp(-x))


def _const_spec(shape):
    nd = len(shape)
    return pl.BlockSpec(shape, lambda *_: (0,) * nd, pipeline_mode=pl.Buffered(1))


def _inproj_kernel(x_ref, n1_ref, w_ref, lbl_ref, rot_ref,
                   qd_ref, kd_ref, vd_ref, qh_ref, ff_ref, fb_ref, ih_ref, gh_ref, sgh_ref, sgd_ref):
    x = x_ref[...]
    ms = jnp.mean(x * x, axis=-1, keepdims=True)
    h = (x * lax.rsqrt(ms + NORM_EPS) * n1_ref[...]).astype(BF16)

    def seg(j):
        return jnp.dot(h, w_ref[:, j * SEG:(j + 1) * SEG], preferred_element_type=F32)

    cos_t = rot_ref[:, 0:LANES]
    sin_lo = rot_ref[:, LANES:2 * LANES]
    sin_hi = rot_ref[:, 2 * LANES:3 * LANES]

    def rope_store(u, out_ref, scale):
        for p in range(SEG // LANES):
            a = u[:, p * LANES:(p + 1) * LANES]
            r = (a * cos_t + pltpu.roll(a, ROT_HALF, 1) * sin_lo
                 + pltpu.roll(a, LANES - ROT_HALF, 1) * sin_hi)
            if scale != 1.0:
                r = r * scale
            out_ref[:, p * LANES:(p + 1) * LANES] = r.astype(out_ref.dtype)

    rope_store(seg(0), qd_ref, DA_SCALE * LOG2_E)
    rope_store(seg(1), kd_ref, 1.0)
    vd_ref[...] = seg(2).astype(BF16)

    u = seg(3)
    qh_ref[...] = (u * _sigmoid(u) * HG_SCALE).astype(BF16)

    def lower_bound(d):
        l0 = lbl_ref[2 * d:2 * d + 1, :]
        l1 = lbl_ref[2 * d + 1:2 * d + 2, :]
        m = jnp.maximum(l0, l1)
        e0 = jnp.exp(l0 - m)
        e1 = jnp.exp(l1 - m)
        return e0 / (e0 + e1)

    lb = lower_bound(0)
    ff_ref[...] = lb + (1.0 - lb) * _sigmoid(seg(4))
    lb = lower_bound(1)
    fb_ref[...] = lb + (1.0 - lb) * _sigmoid(seg(5))
    ih_ref[...] = seg(6).astype(BF16)
    u = seg(7)
    gh_ref[...] = (u * _sigmoid(u)).astype(BF16)
    for p in range(D_MODEL // SEG):
        sgh_ref[:, p * SEG:(p + 1) * SEG] = _sigmoid(seg(8 + p)).astype(BF16)
        sgd_ref[:, p * SEG:(p + 1) * SEG] = _sigmoid(seg(10 + p)).astype(BF16)


def _inproj(x2, n1, w_in, lbl, rot, seq):
    n = x2.shape[0]
    tm = min(IN_ROWS, seq)
    nt = seq // tm
    row = lambda w: pl.BlockSpec((tm, w), lambda i: (i, 0))
    out_w = (DA_WIDTH,) * 3 + (HG_WIDTH,) * 5 + (D_MODEL,) * 2
    out_dt = (BF16, BF16, BF16, BF16, F32, F32, BF16, BF16, BF16, BF16)
    return pl.pallas_call(
        _inproj_kernel,
        grid=(n // tm,),
        in_specs=[row(D_MODEL), _const_spec((1, D_MODEL)), _const_spec((D_MODEL, IN_WIDTH)),
                  _const_spec((4, HG_WIDTH)),
                  pl.BlockSpec((tm, 3 * LANES), lambda i: (i % nt, 0))],
        out_specs=[row(w) for w in out_w],
        out_shape=[jax.ShapeDtypeStruct((n, w), dt) for w, dt in zip(out_w, out_dt)],
        compiler_params=pltpu.CompilerParams(dimension_semantics=("parallel",),
                                             vmem_limit_bytes=VMEM_LIMIT),
        name="inproj",
    )(x2, n1, w_in, lbl, rot)


def _attn_kernel(lam_ref, sub_ref, q_ref, k_ref, v_ref, o_ref, s_ref, p_ref):
    seq = k_ref.shape[0]
    nk = seq // ATT_K
    tiles = ATT_K // LANES

    def bound_check():
        d = lax.broadcasted_iota(jnp.int32, (DA_V_DIM, LANES), 0)
        c = lax.broadcasted_iota(jnp.int32, (DA_V_DIM, LANES), 1)
        pick = jnp.where(c == d // DA_HEAD_DIM, 1.0, 0.0).astype(BF16)

        def max_norm2(ref):
            a = ref[...].astype(F32)
            return jnp.max(jnp.dot((a * a).astype(BF16), pick, preferred_element_type=F32), axis=0, keepdims=True)

        b2 = max_norm2(q_ref) * max_norm2(k_ref)
        b2 = jnp.maximum(b2[:, 0:1], b2[:, 1:2])
        return b2[0, 0] <= 0.98 * ATT_SAFE_LOG2 * ATT_SAFE_LOG2

    lv = lam_ref[...]
    lam = (jnp.exp(jnp.sum(lv[0:1] * lv[1:2], axis=-1, keepdims=True))
           - jnp.exp(jnp.sum(lv[2:3] * lv[3:4], axis=-1, keepdims=True)) + LAM_INIT)
    safe = jnp.logical_and(bound_check(), jnp.abs(lam[0, 0]) <= ATT_SAFE_LAM)
    lane = lax.broadcasted_iota(jnp.int32, (ATT_Q, DA_V_DIM), 1)

    def block_rows(step, x):
        return pl.ds(pl.multiple_of((step * ATT_STREAMS + x) * ATT_Q, ATT_Q), ATT_Q)

    def masked_q(step, x):
        q = q_ref[block_rows(step, x), :]
        zero = jnp.zeros_like(q)
        return jnp.concatenate([jnp.where(lane < DA_HEAD_DIM, q, zero),
                                jnp.where(lane >= DA_HEAD_DIM, q, zero)], axis=0)

    def scores(x, qz, j, mp):
        ks = slice(j * ATT_K, (j + 1) * ATT_K)
        s = lax.dot_general(qz, k_ref[ks, :], _NT, preferred_element_type=F32)
        s_ref[x % 2, :, ks] = s
        for t in range(tiles):
            mp = jnp.maximum(mp, s[:, t * LANES:(t + 1) * LANES])
        return mp

    def exps(x, m, j, lp):
        ks = slice(j * ATT_K, (j + 1) * ATT_K)
        p = jnp.exp2(s_ref[x % 2, :, ks] - m)
        for t in range(tiles):
            lp = lp + p[:, t * LANES:(t + 1) * LANES]
        p_ref[x % 2, :, ks] = p.astype(BF16)
        return lp

    def fused(x, qz, j, lp):
        ks = slice(j * ATT_K, (j + 1) * ATT_K)
        p = jnp.exp2(lax.dot_general(qz, k_ref[ks, :], _NT, preferred_element_type=F32))
        for t in range(tiles):
            lp = lp + p[:, t * LANES:(t + 1) * LANES]
        p_ref[x % 2, :, ks] = p.astype(BF16)
        return lp

    def mix(x, rho, j, acc):
        ks = slice(j * ATT_K, (j + 1) * ATT_K)
        w = p_ref[x % 2, :ATT_Q, ks] - rho * p_ref[x % 2, ATT_Q:, ks]
        return acc + jnp.dot(w, v_ref[ks, :], preferred_element_type=F32)

    def finish(step, x, acc, lp):
        l = jnp.sum(lp, axis=-1, keepdims=True)
        if acc is None:
            return (lam * l[:ATT_Q] / l[ATT_Q:]).astype(BF16)
        o = acc * (1.0 / l[:ATT_Q])
        ms = jnp.mean(o * o, axis=-1, keepdims=True)
        o = o * lax.rsqrt(ms + SUBLN_EPS) * sub_ref[...] * (1.0 - LAM_INIT)
        o_ref[block_rows(step, x), :] = o.astype(o_ref.dtype)

    def fast_step(step, carry):
        lps = [None] * ATT_STREAMS
        for t in range(ATT_STREAMS + 1):
            if t < ATT_STREAMS:
                qz = masked_q(step, t)
                lp = jnp.zeros((2 * ATT_Q, LANES), F32)
                for j in range(nk):
                    lp = fused(t, qz, j, lp)
                lps[t] = lp
            if t >= 1:
                x = t - 1
                rho = finish(step, x, None, lps[x])
                acc = jnp.zeros((ATT_Q, DA_V_DIM), F32)
                for j in range(nk):
                    acc = mix(x, rho, j, acc)
                finish(step, x, acc, lps[x])
        return carry

    def slow_step(step, carry):
        qz = [None] * ATT_STREAMS
        mp = [None] * ATT_STREAMS
        lp = [None] * ATT_STREAMS
        m = [None] * ATT_STREAMS
        rho = [None] * ATT_STREAMS
        acc = [None] * ATT_STREAMS
        for t in range(ATT_STREAMS + 2):
            xs, xe, xm = t, t - 1, t - 2
            if xs < ATT_STREAMS:
                qz[xs] = masked_q(step, xs)
                mp[xs] = jnp.full((2 * ATT_Q, LANES), -jnp.inf, F32)
            if 0 <= xe < ATT_STREAMS:
                m[xe] = jnp.max(mp[xe], axis=-1, keepdims=True)
                lp[xe] = jnp.zeros((2 * ATT_Q, LANES), F32)
            if 0 <= xm < ATT_STREAMS:
                rho[xm] = finish(step, xm, None, lp[xm])
                acc[xm] = jnp.zeros((ATT_Q, DA_V_DIM), F32)
            for j in range(nk):
                if xs < ATT_STREAMS:
                    mp[xs] = scores(xs, qz[xs], j, mp[xs])
                if 0 <= xe < ATT_STREAMS:
                    lp[xe] = exps(xe, m[xe], j, lp[xe])
                if 0 <= xm < ATT_STREAMS:
                    acc[xm] = mix(xm, rho[xm], j, acc[xm])
            if 0 <= xm < ATT_STREAMS:
                finish(step, xm, acc[xm], lp[xm])
        return carry

    steps = seq // (ATT_STREAMS * ATT_Q)

    @pl.when(safe)
    def _():
        lax.fori_loop(0, steps, fast_step, 0)

    @pl.when(jnp.logical_not(safe))
    def _():
        lax.fori_loop(0, steps, slow_step, 0)


def _attention(lamv, subln, qd, kd, vd, batch, seq):
    n = qd.shape[0]
    assert seq % (ATT_STREAMS * ATT_Q) == 0
    slab = pl.BlockSpec((seq, DA_V_DIM), lambda b, h: (b, h))
    return pl.pallas_call(
        _attn_kernel,
        grid=(batch, DA_HEADS),
        in_specs=[pl.BlockSpec((4, DA_HEAD_DIM), lambda b, h: (0, 0)),
                  pl.BlockSpec((1, DA_V_DIM), lambda b, h: (0, 0)),
                  slab, slab, slab],
        out_specs=slab,
        out_shape=jax.ShapeDtypeStruct((n, DA_WIDTH), BF16),
        scratch_shapes=[pltpu.VMEM((2, 2 * ATT_Q, seq), F32),
                        pltpu.VMEM((2, 2 * ATT_Q, seq), BF16)],
        compiler_params=pltpu.CompilerParams(
            dimension_semantics=("parallel", "parallel"), vmem_limit_bytes=VMEM_LIMIT),
        name="diffattn",
    )(lamv, subln, qd, kd, vd)


def _cumsum_rows(x, rev):
    c = x.shape[0]
    row = lax.broadcasted_iota(jnp.int32, x.shape, 0)
    sh = 1
    while sh < c:
        if rev:
            x = x + jnp.where(row < c - sh, pltpu.roll(x, c - sh, 0), 0.0)
        else:
            x = x + jnp.where(row >= sh, pltpu.roll(x, sh, 0), 0.0)
        sh *= 2
    return x


def _block_row(g, half, rev):
    c, w = g.shape
    idx = half if rev else half - 1
    blk = 2 * half
    if blk >= SUBLANES:
        g3 = g.reshape(c // blk, blk, w)
        return jnp.broadcast_to(g3[:, idx:idx + 1, :], g3.shape).reshape(c, w)
    g3 = g.reshape(c // SUBLANES, SUBLANES, w)
    sub = lax.broadcasted_iota(jnp.int32, g3.shape, 1)
    out = None
    for b in reversed(range(SUBLANES // blk)):
        piece = jnp.broadcast_to(g3[:, b * blk + idx:b * blk + idx + 1, :], g3.shape)
        out = piece if out is None else jnp.where(sub < (b + 1) * blk, piece, out)
    return out.reshape(c, w)


def _gla_chunk(q, f, v, st_ref, pair_level, rev):
    c = q.shape[0]
    qf = q.astype(F32)
    vf = v.astype(F32)
    kk = 1.0 - f
    g = _cumsum_rows(jnp.log2(f), rev)
    row = lax.broadcasted_iota(jnp.int32, (c, 1), 0)

    st = st_ref[...]
    o = lax.dot_general((qf * jnp.exp2(g)).astype(BF16), st.astype(BF16), _NT,
                        preferred_element_type=F32)
    o = o + jnp.sum(qf * kk, axis=-1, keepdims=True) * vf

    a = jnp.zeros((c, c), F32)
    half = 1
    lvl = 0
    while half < c:
        second = (row & half) != 0
        tgt = jnp.logical_not(second) if rev else second
        if half == 1:
            d = jnp.where(tgt, f, 1.0)
        else:
            d = jnp.exp2((g - _block_row(g, half, rev)) * jnp.where(tgt, 1.0, -1.0))
        xf = jnp.where(tgt, qf, kk) * d
        x = xf.astype(BF16)
        if half >= SUBLANES:
            first = 0 if rev else half
            pick = lambda m: jnp.concatenate(
                [m[b + first:b + first + half] for b in range(0, c, 2 * half)], axis=0)
            al = lax.dot_general(pick(xf).astype(BF16), x, _NT, preferred_element_type=F32)
            upd = jnp.where(pick(pair_level) == lvl, al, pick(a))
            rows = []
            for i, b in enumerate(range(0, c, 2 * half)):
                other = a[b + half - first:b + 2 * half - first]
                mine = upd[i * half:(i + 1) * half]
                rows += [mine, other] if rev else [other, mine]
            a = jnp.concatenate(rows, axis=0)
        else:
            al = lax.dot_general(x, x, _NT, preferred_element_type=F32)
            a = jnp.where(pair_level == lvl, al, a)
        half *= 2
        lvl += 1
    o = o + jnp.dot(a.astype(BF16), v, preferred_element_type=F32)

    g_tot = g[0:1, :] if rev else g[c - 1:c, :]
    ks = (kk * jnp.exp2(g_tot - g)).astype(BF16)
    st_ref[...] = st * jnp.exp2(g_tot) + lax.dot_general(v, ks, _TN, preferred_element_type=F32)
    return o


def _hgrn_kernel(lv_ref, qf_ref, ff_ref, vf_ref, qb_ref, fb_ref, vb_ref, of_ref, ob_ref, st_ref):
    @pl.when(pl.program_id(1) == 0)
    def _():
        st_ref[...] = jnp.zeros_like(st_ref)

    lv_fwd = lv_ref[0]
    lv_bwd = lv_ref[1]
    for r in range(HG_ROWS):
        for h in range(HG_HEADS):
            sl = slice(h * HG_HEAD_DIM, (h + 1) * HG_HEAD_DIM)
            of_ref[r, :, sl] = _gla_chunk(qf_ref[r, :, sl], ff_ref[r, :, sl], vf_ref[r, :, sl],
                                          st_ref.at[r, h], lv_fwd, False)
            ob_ref[r, :, sl] = _gla_chunk(qb_ref[r, :, sl], fb_ref[r, :, sl], vb_ref[r, :, sl],
                                          st_ref.at[r, HG_HEADS + h], lv_bwd, True)


def _pair_levels(c):
    t = jnp.arange(c, dtype=jnp.int32)[:, None]
    s = jnp.arange(c, dtype=jnp.int32)[None, :]
    x = t ^ s
    lv = jnp.full((c, c), -1, jnp.int32)
    for i in range(c.bit_length() - 1):
        lv = jnp.where((x >> i) == 1, i, lv)
    return jnp.stack([jnp.where(t > s, lv, -1), jnp.where(t < s, lv, -1)])


def _hgrn(qh, ff, fb, ih, batch, seq):
    c = HG_CHUNK
    assert batch % HG_ROWS == 0
    nc = seq // c
    fwd = pl.BlockSpec((HG_ROWS, c, HG_WIDTH), lambda b, j: (b, j, 0))
    bwd = pl.BlockSpec((HG_ROWS, c, HG_WIDTH), lambda b, j: (b, nc - 1 - j, 0))
    qh, ff, fb, ih = (a.reshape(batch, seq, HG_WIDTH) for a in (qh, ff, fb, ih))
    return pl.pallas_call(
        _hgrn_kernel,
        grid=(batch // HG_ROWS, nc),
        in_specs=[pl.BlockSpec((2, c, c), lambda b, j: (0, 0, 0)), fwd, fwd, fwd, bwd, bwd, bwd],
        out_specs=[fwd, bwd],
        out_shape=[jax.ShapeDtypeStruct((batch, seq, HG_WIDTH), F32)] * 2,
        scratch_shapes=[pltpu.VMEM((HG_ROWS, 2 * HG_HEADS, HG_HEAD_DIM, HG_HEAD_DIM), F32)],
        compiler_params=pltpu.CompilerParams(dimension_semantics=("parallel", "arbitrary"),
                                             vmem_limit_bytes=VMEM_LIMIT),
        name="hgrn2",
    )(_pair_levels(c), qh, ff, ih, qh, fb, ih)


def _merge_mlp_kernel(x_ref, of_ref, ob_ref, gh_ref, oa_ref, sgh_ref, sgd_ref,
                      hgn_ref, n2_ref, fn_ref, whg_ref, wda_ref, wout_ref, w1_ref, w2_ref, y_ref):
    o = of_ref[...] + ob_ref[...]
    parts = []
    for h in range(HG_HEADS):
        sl = slice(h * HG_HEAD_DIM, (h + 1) * HG_HEAD_DIM)
        oh = o[:, sl]
        ms = jnp.mean(oh * oh, axis=-1, keepdims=True)
        parts.append(oh * lax.rsqrt(ms + NORM_EPS) * hgn_ref[:, sl])
    o = jnp.concatenate(parts, axis=1) * gh_ref[...].astype(F32)
    y_hg = jnp.dot(o.astype(BF16), whg_ref[...], preferred_element_type=F32)
    y_da = jnp.dot(oa_ref[...], wda_ref[...], preferred_element_type=F32)
    m = sgh_ref[...].astype(F32) * y_hg + sgd_ref[...].astype(F32) * y_da
    x1 = x_ref[...] + jnp.dot(m.astype(BF16), wout_ref[...], preferred_element_type=F32)

    ms = jnp.mean(x1 * x1, axis=-1, keepdims=True)
    h2 = (x1 * lax.rsqrt(ms + NORM_EPS) * n2_ref[...]).astype(BF16)
    acc = x1
    for j in range(D_FF // MLP_COLS):
        cs = slice(j * MLP_COLS, (j + 1) * MLP_COLS)
        mid = jnp.maximum(jnp.dot(h2, w1_ref[:, cs], preferred_element_type=F32), 0.0)
        acc = acc + jnp.dot((mid * mid).astype(BF16), w2_ref[cs, :], preferred_element_type=F32)
    ms = jnp.mean(acc * acc, axis=-1, keepdims=True)
    y_ref[...] = acc * lax.rsqrt(ms + NORM_EPS) * fn_ref[...]


def _merge_mlp(x2, of, ob, gh, oa, sgh, sgd, hgn, n2, fn, whg, wda, wout, w1, w2):
    n = x2.shape[0]
    tm = MLP_ROWS
    row = lambda w: pl.BlockSpec((tm, w), lambda i: (i, 0))
    return pl.pallas_call(
        _merge_mlp_kernel,
        grid=(n // tm,),
        in_specs=[row(D_MODEL), row(HG_WIDTH), row(HG_WIDTH), row(HG_WIDTH), row(DA_WIDTH),
                  row(D_MODEL), row(D_MODEL),
                  _const_spec((1, HG_WIDTH)), _const_spec((1, D_MODEL)), _const_spec((1, D_MODEL)),
                  _const_spec((HG_WIDTH, D_MODEL)), _const_spec((DA_WIDTH, D_MODEL)),
                  _const_spec((D_MODEL, D_MODEL)), _const_spec((D_MODEL, D_FF)),
                  _const_spec((D_FF, D_MODEL))],
        out_specs=row(D_MODEL),
        out_shape=jax.ShapeDtypeStruct((n, D_MODEL), F32),
        compiler_params=pltpu.CompilerParams(dimension_semantics=("parallel",),
                                             vmem_limit_bytes=VMEM_LIMIT),
        name="merge_mlp",
    )(x2, of, ob, gh, oa, sgh, sgd, hgn, n2, fn, whg, wda, wout, w1, w2)


def _rotary_table(seq):
    pos = jnp.arange(seq, dtype=F32)
    inv_freq = ROPE_THETA ** (-jnp.arange(0, ROT_DIM, 2, dtype=F32) / ROT_DIM)
    ang = pos[:, None] * inv_freq[None, :]
    cs = jnp.concatenate([jnp.cos(ang), jnp.sin(ang)], axis=1)
    lane = jnp.arange(3 * LANES)
    table, c = lane // LANES, lane % DA_HEAD_DIM
    src_row = jnp.arange(2 * ROT_HALF)[:, None]
    is_cos = (table == 0) & (c < ROT_DIM) & (src_row == c % ROT_HALF)
    is_lo = (table == 1) & (c >= ROT_HALF) & (c < ROT_DIM) & (src_row == ROT_HALF + c % ROT_HALF)
    is_hi = (table == 2) & (c < ROT_HALF) & (src_row == ROT_HALF + c)
    sel = jnp.where(is_cos | is_lo, 1.0, 0.0) - jnp.where(is_hi, 1.0, 0.0)
    ones = jnp.where((table == 0) & (c >= ROT_DIM), 1.0, 0.0)[None, :]
    return jnp.dot(cs, sel.astype(F32), precision=lax.Precision.HIGHEST) + ones


def _trunk(x, p, rot):
    batch, seq, _ = x.shape
    assert seq % max(IN_ROWS, ATT_STREAMS * ATT_Q, ATT_K, HG_CHUNK, MLP_ROWS) == 0
    x2 = x.reshape(batch * seq, D_MODEL)
    qd, kd, vd, qh, ff, fb, ih, gh, sgh, sgd = _inproj(x2, p["n1"], p["w_in"], p["lbl"], rot, seq)
    oa = _attention(p["lamv"], p["subln"], qd, kd, vd, batch, seq)
    of, ob = (a.reshape(batch * seq, HG_WIDTH) for a in _hgrn(qh, ff, fb, ih, batch, seq))
    y = _merge_mlp(x2, of, ob, gh, oa, sgh, sgd, p["hgn"], p["n2"], p["fn"],
                   p["whg"], p["wda"], p["wout"], p["w1"], p["w2"])
    return y.reshape(batch, seq, D_MODEL)


def kernel(x_prompt, x_sample, norm1, w_in, hg_lb_logits, hg_norm, w_hg_branch, da_lambda_q1, da_lambda_k1, da_lambda_q2, da_lambda_k2, da_subln, w_da_branch, w_out, norm2, w_mlp_in, w_mlp_out, final_norm):
    assert norm1.shape[0] == 1 and hg_lb_logits.shape == (2, 2, HG_WIDTH)
    p = dict(
        n1=norm1.reshape(1, D_MODEL), w_in=w_in[0].astype(BF16),
        lbl=hg_lb_logits.reshape(4, HG_WIDTH),
        lamv=jnp.concatenate([da_lambda_q1, da_lambda_k1, da_lambda_q2, da_lambda_k2], axis=0),
        subln=da_subln.reshape(1, DA_V_DIM), hgn=hg_norm.reshape(1, HG_WIDTH),
        n2=norm2.reshape(1, D_MODEL), fn=final_norm.reshape(1, D_MODEL),
        whg=w_hg_branch[0].astype(BF16), wda=w_da_branch[0].astype(BF16), wout=w_out[0].astype(BF16),
        w1=w_mlp_in[0].astype(BF16), w2=w_mlp_out[0].astype(BF16),
    )
    rot = _rotary_table(max(x_prompt.shape[1], x_sample.shape[1]))
    return _trunk(x_prompt, p, rot), _trunk(x_sample, p, rot)
```

```python
import math

import jax
import jax.numpy as jnp
from jax import lax
from jax.experimental import pallas as pl
from jax.experimental.pallas import tpu as pltpu

F32 = jnp.float32
BF16 = jnp.bfloat16

D_MODEL = 1024
HG_HEADS = 4
HG_HEAD_DIM = 128
HG_WIDTH = HG_HEADS * HG_HEAD_DIM
HG_SCALE = HG_HEAD_DIM ** -0.5
DA_HEADS = 4
DA_HEAD_DIM = 64
DA_V_DIM = 2 * DA_HEAD_DIM
DA_WIDTH = DA_HEADS * DA_V_DIM
DA_SCALE = DA_HEAD_DIM ** -0.5
ROT_DIM = DA_HEAD_DIM // 4
ROT_HALF = ROT_DIM // 2
ROPE_THETA = 500000.0
D_FF = 4 * D_MODEL
NORM_EPS = 1e-6
SUBLN_EPS = 1e-5
IN_WIDTH = 3 * DA_WIDTH + 5 * HG_WIDTH + 2 * D_MODEL
LAM_INIT = 0.8 - 0.6 * math.exp(-0.3 * 0)

LANES = 128
SUBLANES = 8
VMEM_LIMIT = 56 * 1024 * 1024

SEG = 512
IN_ROWS = 1024
ATT_Q = 128
ATT_STREAMS = 16
ATT_K = 512
LOG2_E = math.log2(math.e)
ATT_SAFE_LOG2 = 48.0
ATT_SAFE_LAM = 1024.0
HG_CHUNK = 128
HG_ROWS = 8
MLP_ROWS = 512
MLP_COLS = 1024

_NT = (((1,), (1,)), ((), ()))
_TN = (((0,), (0,)), ((), ()))


def _sigmoid(x):
    return 1.0 / (1.0 + jnp.exp(-x))


def _const_spec(shape):
    nd = len(shape)
    return pl.BlockSpec(shape, lambda *_: (0,) * nd, pipeline_mode=pl.Buffered(1))


def _inproj_kernel(x_ref, n1_ref, w_ref, lbl_ref, rot_ref,
                   qd_ref, kd_ref, vd_ref, qh_ref, ff_ref, fb_ref, ih_ref, gh_ref, sgh_ref, sgd_ref):
    x = x_ref[...]
    ms = jnp.mean(x * x, axis=-1, keepdims=True)
    h = (x * lax.rsqrt(ms + NORM_EPS) * n1_ref[...]).astype(BF16)

    def seg(j):
        return jnp.dot(h, w_ref[:, j * SEG:(j + 1) * SEG], preferred_element_type=F32)

    cos_t = rot_ref[:, 0:LANES]
    sin_lo = rot_ref[:, LANES:2 * LANES]
    sin_hi = rot_ref[:, 2 * LANES:3 * LANES]

    def rope_store(u, out_ref, scale):
        for p in range(SEG // LANES):
            a = u[:, p * LANES:(p + 1) * LANES]
            r = (a * cos_t + pltpu.roll(a, ROT_HALF, 1) * sin_lo
                 + pltpu.roll(a, LANES - ROT_HALF, 1) * sin_hi)
            if scale != 1.0:
                r = r * scale
            out_ref[:, p * LANES:(p + 1) * LANES] = r.astype(out_ref.dtype)

    rope_store(seg(0), qd_ref, DA_SCALE * LOG2_E)
    rope_store(seg(1), kd_ref, 1.0)
    vd_ref[...] = seg(2).astype(BF16)

    u = seg(3)
    qh_ref[...] = (u * _sigmoid(u) * HG_SCALE).astype(BF16)

    def lower_bound(d):
        l0 = lbl_ref[2 * d:2 * d + 1, :]
        l1 = lbl_ref[2 * d + 1:2 * d + 2, :]
        m = jnp.maximum(l0, l1)
        e0 = jnp.exp(l0 - m)
        e1 = jnp.exp(l1 - m)
        return e0 / (e0 + e1)

    lb = lower_bound(0)
    ff_ref[...] = lb + (1.0 - lb) * _sigmoid(seg(4))
    lb = lower_bound(1)
    fb_ref[...] = lb + (1.0 - lb) * _sigmoid(seg(5))
    ih_ref[...] = seg(6).astype(BF16)
    u = seg(7)
    gh_ref[...] = (u * _sigmoid(u)).astype(BF16)
    for p in range(D_MODEL // SEG):
        sgh_ref[:, p * SEG:(p + 1) * SEG] = _sigmoid(seg(8 + p)).astype(BF16)
        sgd_ref[:, p * SEG:(p + 1) * SEG] = _sigmoid(seg(10 + p)).astype(BF16)


def _inproj(x2, n1, w_in, lbl, rot, seq):
    n = x2.shape[0]
    tm = min(IN_ROWS, seq)
    nt = seq // tm
    row = lambda w: pl.BlockSpec((tm, w), lambda i: (i, 0))
    out_w = (DA_WIDTH,) * 3 + (HG_WIDTH,) * 5 + (D_MODEL,) * 2
    out_dt = (BF16, BF16, BF16, BF16, F32, F32, BF16, BF16, BF16, BF16)
    return pl.pallas_call(
        _inproj_kernel,
        grid=(n // tm,),
        in_specs=[row(D_MODEL), _const_spec((1, D_MODEL)), _const_spec((D_MODEL, IN_WIDTH)),
                  _const_spec((4, HG_WIDTH)),
                  pl.BlockSpec((tm, 3 * LANES), lambda i: (i % nt, 0))],
        out_specs=[row(w) for w in out_w],
        out_shape=[jax.ShapeDtypeStruct((n, w), dt) for w, dt in zip(out_w, out_dt)],
        compiler_params=pltpu.CompilerParams(dimension_semantics=("parallel",),
                                             vmem_limit_bytes=VMEM_LIMIT),
        name="inproj",
    )(x2, n1, w_in, lbl, rot)


def _attn_kernel(lam_ref, sub_ref, q_ref, k_ref, v_ref, o_ref, s_ref, p_ref):
    seq = k_ref.shape[0]
    nk = seq // ATT_K
    tiles = ATT_K // LANES

    def bound_check():
        d = lax.broadcasted_iota(jnp.int32, (DA_V_DIM, LANES), 0)
        c = lax.broadcasted_iota(jnp.int32, (DA_V_DIM, LANES), 1)
        pick = jnp.where(c == d // DA_HEAD_DIM, 1.0, 0.0).astype(BF16)

        def max_norm2(ref):
            a = ref[...].astype(F32)
            return jnp.max(jnp.dot((a * a).astype(BF16), pick, preferred_element_type=F32), axis=0, keepdims=True)

        b2 = max_norm2(q_ref) * max_norm2(k_ref)
        b2 = jnp.maximum(b2[:, 0:1], b2[:, 1:2])
        return b2[0, 0] <= 0.98 * ATT_SAFE_LOG2 * ATT_SAFE_LOG2

    lv = lam_ref[...]
    lam = (jnp.exp(jnp.sum(lv[0:1] * lv[1:2], axis=-1, keepdims=True))
           - jnp.exp(jnp.sum(lv[2:3] * lv[3:4], axis=-1, keepdims=True)) + LAM_INIT)
    safe = jnp.logical_and(bound_check(), jnp.abs(lam[0, 0]) <= ATT_SAFE_LAM)
    lane = lax.broadcasted_iota(jnp.int32, (ATT_Q, DA_V_DIM), 1)

    def block_rows(step, x):
        return pl.ds(pl.multiple_of((step * ATT_STREAMS + x) * ATT_Q, ATT_Q), ATT_Q)

    def masked_q(step, x):
        q = q_ref[block_rows(step, x), :]
        zero = jnp.zeros_like(q)
        return jnp.concatenate([jnp.where(lane < DA_HEAD_DIM, q, zero),
                                jnp.where(lane >= DA_HEAD_DIM, q, zero)], axis=0)

    def scores(x, qz, j, mp):
        ks = slice(j * ATT_K, (j + 1) * ATT_K)
        s = lax.dot_general(qz, k_ref[ks, :], _NT, preferred_element_type=F32)
        s_ref[x % 2, :, ks] = s
        for t in range(tiles):
            mp = jnp.maximum(mp, s[:, t * LANES:(t + 1) * LANES])
        return mp

    def exps(x, m, j, lp):
        ks = slice(j * ATT_K, (j + 1) * ATT_K)
        p = jnp.exp2(s_ref[x % 2, :, ks] - m)
        for t in range(tiles):
            lp = lp + p[:, t * LANES:(t + 1) * LANES]
        p_ref[x % 2, :, ks] = p.astype(BF16)
        return lp

    def fused(x, qz, j, lp):
        ks = slice(j * ATT_K, (j + 1) * ATT_K)
        p = jnp.exp2(lax.dot_general(qz, k_ref[ks, :], _NT, preferred_element_type=F32))
        for t in range(tiles):
            lp = lp + p[:, t * LANES:(t + 1) * LANES]
        p_ref[x % 2, :, ks] = p.astype(BF16)
        return lp

    def mix(x, rho, j, acc):
        ks = slice(j * ATT_K, (j + 1) * ATT_K)
        w = p_ref[x % 2, :ATT_Q, ks] - rho * p_ref[x % 2, ATT_Q:, ks]
        return acc + jnp.dot(w, v_ref[ks, :], preferred_element_type=F32)

    def finish(step, x, acc, lp):
        l = jnp.sum(lp, axis=-1, keepdims=True)
        if acc is None:
            return (lam * l[:ATT_Q] / l[ATT_Q:]).astype(BF16)
        o = acc * (1.0 / l[:ATT_Q])
        ms = jnp.mean(o * o, axis=-1, keepdims=True)
        o = o * lax.rsqrt(ms + SUBLN_EPS) * sub_ref[...] * (1.0 - LAM_INIT)
        o_ref[block_rows(step, x), :] = o.astype(o_ref.dtype)

    def fast_step(step, carry):
        lps = [None] * ATT_STREAMS
        for t in range(ATT_STREAMS + 1):
            if t < ATT_STREAMS:
                qz = masked_q(step, t)
                lp = jnp.zeros((2 * ATT_Q, LANES), F32)
                for j in range(nk):
                    lp = fused(t, qz, j, lp)
                lps[t] = lp
            if t >= 1:
                x = t - 1
                rho = finish(step, x, None, lps[x])
                acc = jnp.zeros((ATT_Q, DA_V_DIM), F32)
                for j in range(nk):
                    acc = mix(x, rho, j, acc)
                finish(step, x, acc, lps[x])
        return carry

    def slow_step(step, carry):
        qz = [None] * ATT_STREAMS
        mp = [None] * ATT_STREAMS
        lp = [None] * ATT_STREAMS
        m = [None] * ATT_STREAMS
        rho = [None] * ATT_STREAMS
        acc = [None] * ATT_STREAMS
        for t in range(ATT_STREAMS + 2):
            xs, xe, xm = t, t - 1, t - 2
            if xs < ATT_STREAMS:
                qz[xs] = masked_q(step, xs)
                mp[xs] = jnp.full((2 * ATT_Q, LANES), -jnp.inf, F32)
            if 0 <= xe < ATT_STREAMS:
                m[xe] = jnp.max(mp[xe], axis=-1, keepdims=True)
                lp[xe] = jnp.zeros((2 * ATT_Q, LANES), F32)
            if 0 <= xm < ATT_STREAMS:
                rho[xm] = finish(step, xm, None, lp[xm])
                acc[xm] = jnp.zeros((ATT_Q, DA_V_DIM), F32)
            for j in range(nk):
                if xs < ATT_STREAMS:
                    mp[xs] = scores(xs, qz[xs], j, mp[xs])
                if 0 <= xe < ATT_STREAMS:
                    lp[xe] = exps(xe, m[xe], j, lp[xe])
                if 0 <= xm < ATT_STREAMS:
                    acc[xm] = mix(xm, rho[xm], j, acc[xm])
            if 0 <= xm < ATT_STREAMS:
                finish(step, xm, acc[xm], lp[xm])
        return carry

    steps = seq // (ATT_STREAMS * ATT_Q)

    @pl.when(safe)
    def _():
        lax.fori_loop(0, steps, fast_step, 0)

    @pl.when(jnp.logical_not(safe))
    def _():
        lax.fori_loop(0, steps, slow_step, 0)


def _attention(lamv, subln, qd, kd, vd, batch, seq):
    n = qd.shape[0]
    assert seq % (ATT_STREAMS * ATT_Q) == 0
    slab = pl.BlockSpec((seq, DA_V_DIM), lambda b, h: (b, h))
    return pl.pallas_call(
        _attn_kernel,
        grid=(batch, DA_HEADS),
        in_specs=[pl.BlockSpec((4, DA_HEAD_DIM), lambda b, h: (0, 0)),
                  pl.BlockSpec((1, DA_V_DIM), lambda b, h: (0, 0)),
                  slab, slab, slab],
        out_specs=slab,
        out_shape=jax.ShapeDtypeStruct((n, DA_WIDTH), BF16),
        scratch_shapes=[pltpu.VMEM((2, 2 * ATT_Q, seq), F32),
                        pltpu.VMEM((2, 2 * ATT_Q, seq), BF16)],
        compiler_params=pltpu.CompilerParams(
            dimension_semantics=("parallel", "parallel"), vmem_limit_bytes=VMEM_LIMIT),
        name="diffattn",
    )(lamv, subln, qd, kd, vd)


def _cumsum_rows(x, rev):
    c = x.shape[0]
    row = lax.broadcasted_iota(jnp.int32, x.shape, 0)
    sh = 1
    while sh < c:
        if rev:
            x = x + jnp.where(row < c - sh, pltpu.roll(x, c - sh, 0), 0.0)
        else:
            x = x + jnp.where(row >= sh, pltpu.roll(x, sh, 0), 0.0)
        sh *= 2
    return x


def _block_row(g, half, rev):
    c, w = g.shape
    idx = half if rev else half - 1
    blk = 2 * half
    if blk >= SUBLANES:
        g3 = g.reshape(c // blk, blk, w)
        return jnp.broadcast_to(g3[:, idx:idx + 1, :], g3.shape).reshape(c, w)
    g3 = g.reshape(c // SUBLANES, SUBLANES, w)
    sub = lax.broadcasted_iota(jnp.int32, g3.shape, 1)
    out = None
    for b in reversed(range(SUBLANES // blk)):
        piece = jnp.broadcast_to(g3[:, b * blk + idx:b * blk + idx + 1, :], g3.shape)
        out = piece if out is None else jnp.where(sub < (b + 1) * blk, piece, out)
    return out.reshape(c, w)


def _gla_chunk(q, f, v, st_ref, pair_level, rev):
    c = q.shape[0]
    qf = q.astype(F32)
    vf = v.astype(F32)
    kk = 1.0 - f
    g = _cumsum_rows(jnp.log2(f), rev)
    row = lax.broadcasted_iota(jnp.int32, (c, 1), 0)

    st = st_ref[...]
    o = lax.dot_general((qf * jnp.exp2(g)).astype(BF16), st.astype(BF16), _NT,
                        preferred_element_type=F32)
    o = o + jnp.sum(qf * kk, axis=-1, keepdims=True) * vf

    a = jnp.zeros((c, c), F32)
    half = 1
    lvl = 0
    while half < c:
        second = (row & half) != 0
        tgt = jnp.logical_not(second) if rev else second
        if half == 1:
            d = jnp.where(tgt, f, 1.0)
        else:
            d = jnp.exp2((g - _block_row(g, half, rev)) * jnp.where(tgt, 1.0, -1.0))
        xf = jnp.where(tgt, qf, kk) * d
        x = xf.astype(BF16)
        if half >= SUBLANES:
            first = 0 if rev else half
            pick = lambda m: jnp.concatenate(
                [m[b + first:b + first + half] for b in range(0, c, 2 * half)], axis=0)
            al = lax.dot_general(pick(xf).astype(BF16), x, _NT, preferred_element_type=F32)
            upd = jnp.where(pick(pair_level) == lvl, al, pick(a))
            rows = []
            for i, b in enumerate(range(0, c, 2 * half)):
                other = a[b + half - first:b + 2 * half - first]
                mine = upd[i * half:(i + 1) * half]
                rows += [mine, other] if rev else [other, mine]
            a = jnp.concatenate(rows, axis=0)
        else:
            al = lax.dot_general(x, x, _NT, preferred_element_type=F32)
            a = jnp.where(pair_level == lvl, al, a)
        half *= 2
        lvl += 1
    o = o + jnp.dot(a.astype(BF16), v, preferred_element_type=F32)

    g_tot = g[0:1, :] if rev else g[c - 1:c, :]
    ks = (kk * jnp.exp2(g_tot - g)).astype(BF16)
    st_ref[...] = st * jnp.exp2(g_tot) + lax.dot_general(v, ks, _TN, preferred_element_type=F32)
    return o


def _hgrn_kernel(lv_ref, qf_ref, ff_ref, vf_ref, qb_ref, fb_ref, vb_ref, of_ref, ob_ref, st_ref):
    @pl.when(pl.program_id(1) == 0)
    def _():
        st_ref[...] = jnp.zeros_like(st_ref)

    lv_fwd = lv_ref[0]
    lv_bwd = lv_ref[1]
    for r in range(HG_ROWS):
        for h in range(HG_HEADS):
            sl = slice(h * HG_HEAD_DIM, (h + 1) * HG_HEAD_DIM)
            of_ref[r, :, sl] = _gla_chunk(qf_ref[r, :, sl], ff_ref[r, :, sl], vf_ref[r, :, sl],
                                          st_ref.at[r, h], lv_fwd, False)
            ob_ref[r, :, sl] = _gla_chunk(qb_ref[r, :, sl], fb_ref[r, :, sl], vb_ref[r, :, sl],
                                          st_ref.at[r, HG_HEADS + h], lv_bwd, True)


def _pair_levels(c):
    t = jnp.arange(c, dtype=jnp.int32)[:, None]
    s = jnp.arange(c, dtype=jnp.int32)[None, :]
    x = t ^ s
    lv = jnp.full((c, c), -1, jnp.int32)
    for i in range(c.bit_length() - 1):
        lv = jnp.where((x >> i) == 1, i, lv)
    return jnp.stack([jnp.where(t > s, lv, -1), jnp.where(t < s, lv, -1)])


def _hgrn(qh, ff, fb, ih, batch, seq):
    c = HG_CHUNK
    assert batch % HG_ROWS == 0
    nc = seq // c
    fwd = pl.BlockSpec((HG_ROWS, c, HG_WIDTH), lambda b, j: (b, j, 0))
    bwd = pl.BlockSpec((HG_ROWS, c, HG_WIDTH), lambda b, j: (b, nc - 1 - j, 0))
    qh, ff, fb, ih = (a.reshape(batch, seq, HG_WIDTH) for a in (qh, ff, fb, ih))
    return pl.pallas_call(
        _hgrn_kernel,
        grid=(batch // HG_ROWS, nc),
        in_specs=[pl.BlockSpec((2, c, c), lambda b, j: (0, 0, 0)), fwd, fwd, fwd, bwd, bwd, bwd],
        out_specs=[fwd, bwd],
        out_shape=[jax.ShapeDtypeStruct((batch, seq, HG_WIDTH), F32)] * 2,
        scratch_shapes=[pltpu.VMEM((HG_ROWS, 2 * HG_HEADS, HG_HEAD_DIM, HG_HEAD_DIM), F32)],
        compiler_params=pltpu.CompilerParams(dimension_semantics=("parallel", "arbitrary"),
                                             vmem_limit_bytes=VMEM_LIMIT),
        name="hgrn2",
    )(_pair_levels(c), qh, ff, ih, qh, fb, ih)


def _merge_mlp_kernel(x_ref, of_ref, ob_ref, gh_ref, oa_ref, sgh_ref, sgd_ref,
                      hgn_ref, n2_ref, fn_ref, whg_ref, wda_ref, wout_ref, w1_ref, w2_ref, y_ref):
    half = x_ref.shape[0] // 2
    for r in range(2):
        rows = slice(r * half, (r + 1) * half)
        o = of_ref[rows, :] + ob_ref[rows, :]
        parts = []
        for h in range(HG_HEADS):
            sl = slice(h * HG_HEAD_DIM, (h + 1) * HG_HEAD_DIM)
            oh = o[:, sl]
            ms = jnp.mean(oh * oh, axis=-1, keepdims=True)
            parts.append(oh * lax.rsqrt(ms + NORM_EPS) * hgn_ref[:, sl])
        o = jnp.concatenate(parts, axis=1) * gh_ref[rows, :].astype(F32)
        y_hg = jnp.dot(o.astype(BF16), whg_ref[...], preferred_element_type=F32)
        y_da = jnp.dot(oa_ref[rows, :], wda_ref[...], preferred_element_type=F32)
        m = sgh_ref[rows, :].astype(F32) * y_hg + sgd_ref[rows, :].astype(F32) * y_da
        x1 = x_ref[rows, :] + jnp.dot(m.astype(BF16), wout_ref[...], preferred_element_type=F32)

        ms = jnp.mean(x1 * x1, axis=-1, keepdims=True)
        h2 = (x1 * lax.rsqrt(ms + NORM_EPS) * n2_ref[...]).astype(BF16)
        acc = x1
        for j in range(D_FF // MLP_COLS):
            cs = slice(j * MLP_COLS, (j + 1) * MLP_COLS)
            mid = jnp.maximum(jnp.dot(h2, w1_ref[:, cs], preferred_element_type=F32), 0.0)
            acc = acc + jnp.dot((mid * mid).astype(BF16), w2_ref[cs, :], preferred_element_type=F32)
        ms = jnp.mean(acc * acc, axis=-1, keepdims=True)
        y_ref[rows, :] = acc * lax.rsqrt(ms + NORM_EPS) * fn_ref[...]


def _merge_mlp(x2, of, ob, gh, oa, sgh, sgd, hgn, n2, fn, whg, wda, wout, w1, w2):
    n = x2.shape[0]
    tm = MLP_ROWS
    row = lambda w: pl.BlockSpec((tm, w), lambda i: (i, 0))
    return pl.pallas_call(
        _merge_mlp_kernel,
        grid=(n // tm,),
        in_specs=[row(D_MODEL), row(HG_WIDTH), row(HG_WIDTH), row(HG_WIDTH), row(DA_WIDTH),
                  row(D_MODEL), row(D_MODEL),
                  _const_spec((1, HG_WIDTH)), _const_spec((1, D_MODEL)), _const_spec((1, D_MODEL)),
                  _const_spec((HG_WIDTH, D_MODEL)), _const_spec((DA_WIDTH, D_MODEL)),
                  _const_spec((D_MODEL, D_MODEL)), _const_spec((D_MODEL, D_FF)),
                  _const_spec((D_FF, D_MODEL))],
        out_specs=row(D_MODEL),
        out_shape=jax.ShapeDtypeStruct((n, D_MODEL), F32),
        compiler_params=pltpu.CompilerParams(dimension_semantics=("parallel",),
                                             vmem_limit_bytes=VMEM_LIMIT),
        name="merge_mlp",
    )(x2, of, ob, gh, oa, sgh, sgd, hgn, n2, fn, whg, wda, wout, w1, w2)


def _rotary_table(seq):
    pos = jnp.arange(seq, dtype=F32)
    inv_freq = ROPE_THETA ** (-jnp.arange(0, ROT_DIM, 2, dtype=F32) / ROT_DIM)
    ang = pos[:, None] * inv_freq[None, :]
    cs = jnp.concatenate([jnp.cos(ang), jnp.sin(ang)], axis=1)
    lane = jnp.arange(3 * LANES)
    table, c = lane // LANES, lane % DA_HEAD_DIM
    src_row = jnp.arange(2 * ROT_HALF)[:, None]
    is_cos = (table == 0) & (c < ROT_DIM) & (src_row == c % ROT_HALF)
    is_lo = (table == 1) & (c >= ROT_HALF) & (c < ROT_DIM) & (src_row == ROT_HALF + c % ROT_HALF)
    is_hi = (table == 2) & (c < ROT_HALF) & (src_row == ROT_HALF + c)
    sel = jnp.where(is_cos | is_lo, 1.0, 0.0) - jnp.where(is_hi, 1.0, 0.0)
    ones = jnp.where((table == 0) & (c >= ROT_DIM), 1.0, 0.0)[None, :]
    return jnp.dot(cs, sel.astype(F32), precision=lax.Precision.HIGHEST) + ones


def _trunk(x, p, rot):
    batch, seq, _ = x.shape
    assert seq % max(IN_ROWS, ATT_STREAMS * ATT_Q, ATT_K, HG_CHUNK, MLP_ROWS) == 0
    x2 = x.reshape(batch * seq, D_MODEL)
    qd, kd, vd, qh, ff, fb, ih, gh, sgh, sgd = _inproj(x2, p["n1"], p["w_in"], p["lbl"], rot, seq)
    oa = _attention(p["lamv"], p["subln"], qd, kd, vd, batch, seq)
    of, ob = (a.reshape(batch * seq, HG_WIDTH) for a in _hgrn(qh, ff, fb, ih, batch, seq))
    y = _merge_mlp(x2, of, ob, gh, oa, sgh, sgd, p["hgn"], p["n2"], p["fn"],
                   p["whg"], p["wda"], p["wout"], p["w1"], p["w2"])
    return y.reshape(batch, seq, D_MODEL)


def kernel(x_prompt, x_sample, norm1, w_in, hg_lb_logits, hg_norm, w_hg_branch, da_lambda_q1, da_lambda_k1, da_lambda_q2, da_lambda_k2, da_subln, w_da_branch, w_out, norm2, w_mlp_in, w_mlp_out, final_norm):
    assert norm1.shape[0] == 1 and hg_lb_logits.shape == (2, 2, HG_WIDTH)
    p = dict(
        n1=norm1.reshape(1, D_MODEL), w_in=w_in[0].astype(BF16),
        lbl=hg_lb_logits.reshape(4, HG_WIDTH),
        lamv=jnp.concatenate([da_lambda_q1, da_lambda_k1, da_lambda_q2, da_lambda_k2], axis=0),
        subln=da_subln.reshape(1, DA_V_DIM), hgn=hg_norm.reshape(1, HG_WIDTH),
        n2=norm2.reshape(1, D_MODEL), fn=final_norm.reshape(1, D_MODEL),
        whg=w_hg_branch[0].astype(BF16), wda=w_da_branch[0].astype(BF16), wout=w_out[0].astype(BF16),
        w1=w_mlp_in[0].astype(BF16), w2=w_mlp_out[0].astype(BF16),
    )
    rot = _rotary_table(max(x_prompt.shape[1], x_sample.shape[1]))
    return _trunk(x_prompt, p, rot), _trunk(x_sample, p, rot)
```
